```python
import math
import jax, jax.numpy as jnp
from jax import lax
import numpy as np

D_MODEL = 2048
BATCH = 4
SEQ = 2048
DEPTH = 4

D_MIX = D_MODEL
RET_HEADS = 4
RET_DIM = D_MIX // 4
RET_HEAD_DIM = RET_DIM // RET_HEADS
RET_CHUNK = 128
ROPE_BASE = 10000.0
HG_HEADS = 4
HG_DIM = D_MIX // 4
HG_HEAD_DIM = HG_DIM // HG_HEADS
HG_CHUNK = 64
M_DIM = D_MIX - RET_DIM - HG_DIM
M_HEAD_DIM = 64
M_HEADS = M_DIM // M_HEAD_DIM
M_GROUPS = 2
M_STATE = 128
M_CONV = 4
M_CHUNK = 128
M_CONV_DIM = M_DIM + 2 * M_GROUPS * M_STATE
D_FF = 4 * D_MODEL
DEEPNORM_ALPHA = (2 * DEPTH) ** 0.25
DEEPNORM_BETA = (8 * DEPTH) ** -0.25
LN_EPS = 1e-5
RMS_EPS = 1e-6
N_ADA = 6
IN_SPLITS = (RET_DIM, RET_DIM, RET_DIM, RET_DIM,
             HG_DIM, HG_DIM, HG_DIM, HG_DIM,
             M_DIM, M_CONV_DIM, M_HEADS)
IN_COLS = 4 * RET_DIM + 4 * HG_DIM + M_DIM + M_CONV_DIM + M_HEADS

kernel_name = "hymba_style_ret_hgrn2_ssd_deepnorm_adaln"


def layer_norm(x, g, b):
    xf = x.astype(jnp.float32)
    mu = jnp.mean(xf, axis=-1, keepdims=True)
    var = jnp.mean(jnp.square(xf - mu), axis=-1, keepdims=True)
    return ((xf - mu) * lax.rsqrt(var + LN_EPS) * g + b).astype(x.dtype)


def rms_norm(x):
    xf = x.astype(jnp.float32)
    return xf * lax.rsqrt(jnp.mean(jnp.square(xf), axis=-1, keepdims=True) + RMS_EPS)


def rotary(positions):
    inv_freq = 1.0 / (ROPE_BASE ** jnp.linspace(0.0, 1.0, RET_HEAD_DIM // 2, dtype=jnp.float32))
    ang = positions[..., None].astype(jnp.float32) * inv_freq
    return jnp.cos(ang)[:, :, None, :], jnp.sin(ang)[:, :, None, :]


def apply_rotary(t, cos, sin):
    half = t.shape[-1] // 2
    t1, t2 = t[..., :half], t[..., half:]
    return jnp.concatenate([t1 * cos - t2 * sin, t1 * sin + t2 * cos], axis=-1)


def chunked_scalar_decay(q, k, v, log_a, chunk):
    b, l, h, n = q.shape
    p = v.shape[-1]
    nc = l // chunk
    q = q.reshape(b, nc, chunk, h, n)
    k = k.reshape(b, nc, chunk, h, n)
    v = v.reshape(b, nc, chunk, h, p)
    cum = jnp.cumsum(log_a.astype(jnp.float32).reshape(b, nc, chunk, h), axis=2)
    mask = jnp.tril(jnp.ones((chunk, chunk), bool))[None, None, :, :, None]
    seg = cum[:, :, :, None, :] - cum[:, :, None, :, :]
    decay = jnp.exp(jnp.where(mask, seg, -jnp.inf))
    scores = jnp.einsum('bcihn,bcjhn->bcijh', q, k) * decay
    y_intra = jnp.einsum('bcijh,bcjhp->bcihp', scores, v)
    k_end = k * jnp.exp(cum[:, :, -1:, :] - cum)[..., None]
    states = jnp.einsum('bcjhn,bcjhp->bchnp', k_end, v)
    chunk_decay = jnp.exp(cum[:, :, -1, :])

    def step(s, inp):
        st, dc = inp
        return dc[..., None, None] * s + st, s

    s0 = jnp.zeros((b, h, n, p), jnp.float32)
    _, s_prev = lax.scan(step, s0, (jnp.moveaxis(states, 1, 0), jnp.moveaxis(chunk_decay, 1, 0)))
    s_prev = jnp.moveaxis(s_prev, 0, 1)
    y_inter = jnp.einsum('bcihn,bchnp->bcihp', q * jnp.exp(cum)[..., None], s_prev)
    return (y_intra + y_inter).reshape(b, l, h, p)


def chunked_vector_decay(q, k, v, log_f, chunk):
    b, l, h, dk = q.shape
    dv = v.shape[-1]
    nc = l // chunk

    def to_chunks(t):
        return jnp.moveaxis(t.reshape(b, nc, chunk, *t.shape[2:]), 1, 0)

    mask = jnp.tril(jnp.ones((chunk, chunk), bool))[None, :, :, None, None]

    def step(state, inp):
        qc, kc, vc, gc = inp
        cum = jnp.cumsum(gc, axis=1)
        rel = jnp.exp(jnp.where(mask, cum[:, :, None] - cum[:, None], -jnp.inf))
        scores = jnp.einsum('bihd,bjhd,bijhd->bijh', qc, kc, rel)
        y = (jnp.einsum('bijh,bjhv->bihv', scores, vc)
             + jnp.einsum('bihd,bhdv->bihv', qc * jnp.exp(cum), state))
        total = cum[:, -1]
        state = (jnp.exp(total)[..., None] * state
                 + jnp.einsum('bjhd,bjhv->bhdv', kc * jnp.exp(total[:, None] - cum), vc))
        return state, y

    s0 = jnp.zeros((b, h, dk, dv), jnp.float32)
    _, y = lax.scan(step, s0, (to_chunks(q), to_chunks(k), to_chunks(v),
                               to_chunks(log_f.astype(jnp.float32))))
    return jnp.moveaxis(y, 0, 1).reshape(b, l, h, dv)


def retention(q, k, v, g, positions):
    b, l, _ = q.shape
    q = q.reshape(b, l, RET_HEADS, RET_HEAD_DIM)
    k = k.reshape(b, l, RET_HEADS, RET_HEAD_DIM)
    v = v.reshape(b, l, RET_HEADS, RET_HEAD_DIM)
    cos, sin = rotary(positions)
    q = apply_rotary(q, cos, sin)
    k = apply_rotary(k, cos, sin) * (RET_HEAD_DIM ** -0.5)
    log_gamma = jnp.log(1.0 - jnp.exp2(-5.0 - jnp.arange(RET_HEADS, dtype=jnp.float32)))
    log_a = jnp.broadcast_to(log_gamma, (b, l, RET_HEADS))
    o = rms_norm(chunked_scalar_decay(q, k, v, log_a, RET_CHUNK))
    return jax.nn.silu(g.astype(jnp.float32)) * o.reshape(b, l, RET_DIM)


def hgrn2(q, f_raw, i, g, lb, norm_w):
    b, l, _ = q.shape
    f = lb + (1.0 - lb) * jax.nn.sigmoid(f_raw.astype(jnp.float32))
    log_f = jnp.log(f)
    k = 1.0 - f
    shp = (b, l, HG_HEADS, HG_HEAD_DIM)
    o = chunked_vector_decay(q.reshape(shp), k.reshape(shp), i.reshape(shp), log_f.reshape(shp), HG_CHUNK)
    o = rms_norm(o).reshape(b, l, HG_DIM) * norm_w
    return o * jax.nn.silu(g.astype(jnp.float32))


def causal_conv(u, w, bias):
    out = lax.conv_general_dilated(u, w.astype(u.dtype)[:, None, :], window_strides=(1,),
                                   padding=[(M_CONV - 1, 0)],
                                   dimension_numbers=('NWC', 'WIO', 'NWC'),
                                   feature_group_count=u.shape[-1])
    return out + bias


def mamba2(z, xbc, dt_raw, conv_w, conv_b, dt_bias, a_log, d_skip, norm_w):
    b, l, _ = z.shape
    xbc = jax.nn.silu(causal_conv(xbc, conv_w, conv_b))
    xs, bm, cm = jnp.split(xbc, [M_DIM, M_DIM + M_GROUPS * M_STATE], axis=-1)
    xs = xs.reshape(b, l, M_HEADS, M_HEAD_DIM)
    rep = M_HEADS // M_GROUPS
    bm = jnp.repeat(bm.reshape(b, l, M_GROUPS, M_STATE), rep, axis=2)
    cm = jnp.repeat(cm.reshape(b, l, M_GROUPS, M_STATE), rep, axis=2)
    dt = jax.nn.softplus(dt_raw.astype(jnp.float32) + dt_bias)
    a = -jnp.exp(a_log.astype(jnp.float32))
    y = chunked_scalar_decay(cm, bm, xs * dt[..., None], dt * a, M_CHUNK)
    y = y + xs * d_skip[:, None]
    y = y.reshape(b, l, M_DIM) * jax.nn.silu(z.astype(jnp.float32))
    y = rms_norm(y.reshape(b, l, M_GROUPS, M_DIM // M_GROUPS)).reshape(b, l, M_DIM)
    return y * norm_w


def hybrid_mixer(h, positions, lb, w_in, w_out, hg_norm_w, conv_w, conv_b, dt_bias, a_log, d_skip, m_norm_w):
    proj = h @ w_in
    idx = [int(s) for s in np.cumsum(IN_SPLITS)[:-1]]
    rq, rk, rv, rg, hq, hf, hi, hg, mz, mxbc, mdt = jnp.split(proj, idx, axis=-1)
    o_ret = retention(rq, rk, rv, rg, positions)
    o_hg = hgrn2(hq, hf, hi, hg, lb, hg_norm_w)
    o_m = mamba2(mz, mxbc, mdt, conv_w, conv_b, dt_bias, a_log, d_skip, m_norm_w)
    o = jnp.concatenate([o_ret, o_hg, o_m], axis=-1).astype(h.dtype)
    return o @ w_out


def setup_inputs(seed: int = 0) -> dict:
    key = jax.random.key(seed)
    ks = jax.random.split(key, 20)
    f32 = jnp.float32
    nrm = lambda k, shp, s: jax.random.normal(k, shp, f32) * s
    dt0 = jnp.exp(jax.random.uniform(ks[12], (DEPTH, M_HEADS), f32, math.log(1e-3), math.log(1e-1)))
    return {
        "x": nrm(ks[0], (BATCH, SEQ, D_MODEL), 1.0),
        "c": nrm(ks[1], (BATCH, D_MODEL), 1.0),
        "positions": jnp.tile(jnp.arange(SEQ, dtype=jnp.int32)[None], (BATCH, 1)),
        "lb_logits": nrm(ks[2], (DEPTH, HG_DIM), 0.1),
        "w_in": nrm(ks[3], (DEPTH, D_MODEL, IN_COLS), D_MODEL ** -0.5),
        "w_out": nrm(ks[4], (DEPTH, D_MIX, D_MODEL), D_MIX ** -0.5 * DEEPNORM_BETA),
        "w_ada": nrm(ks[5], (DEPTH, D_MODEL, N_ADA * D_MODEL), 0.3 * D_MODEL ** -0.5),
        "b_ada": nrm(ks[6], (DEPTH, N_ADA * D_MODEL), 0.02),
        "ln_g": 1.0 + nrm(ks[7], (DEPTH, 2, D_MODEL), 0.02),
        "ln_b": nrm(ks[8], (DEPTH, 2, D_MODEL), 0.02),
        "hg_norm_w": 1.0 + nrm(ks[9], (DEPTH, HG_DIM), 0.02),
        "m_conv_w": nrm(ks[10], (DEPTH, M_CONV, M_CONV_DIM), M_CONV ** -0.5),
        "m_conv_b": nrm(ks[11], (DEPTH, M_CONV_DIM), 0.02),
        "m_dt_bias": dt0 + jnp.log(-jnp.expm1(-dt0)),
        "m_a_log": jnp.log(jax.random.uniform(ks[13], (DEPTH, M_HEADS), f32, 1.0, 16.0)),
        "m_d": 1.0 + nrm(ks[14], (DEPTH, M_HEADS), 0.02),
        "m_norm_w": 1.0 + nrm(ks[15], (DEPTH, M_DIM), 0.02),
        "w1": nrm(ks[16], (DEPTH, D_MODEL, D_FF), D_MODEL ** -0.5),
        "w2": nrm(ks[17], (DEPTH, D_FF, D_MODEL), D_FF ** -0.5 * DEEPNORM_BETA),
    }


def reference(x, c, positions, lb_logits, w_in, w_out, w_ada, b_ada, ln_g, ln_b, hg_norm_w,
              m_conv_w, m_conv_b, m_dt_bias, m_a_log, m_d, m_norm_w, w1, w2):
    p = jax.nn.softmax(lb_logits.astype(jnp.float32), axis=0)
    lbs = jnp.cumsum(p, axis=0) - p[0]
    cond = jax.nn.silu(c)
    for l in range(DEPTH):
        ada = (cond @ w_ada[l] + b_ada[l])[:, None, :]
        sh_m, sc_m, g_m, sh_f, sc_f, g_f = jnp.split(ada, N_ADA, axis=-1)
        h = x * (1.0 + sc_m) + sh_m
        y = hybrid_mixer(h, positions, lbs[l], w_in[l], w_out[l], hg_norm_w[l], m_conv_w[l],
                         m_conv_b[l], m_dt_bias[l], m_a_log[l], m_d[l], m_norm_w[l])
        x = layer_norm(DEEPNORM_ALPHA * x + (1.0 + g_m) * y, ln_g[l, 0], ln_b[l, 0])
        h = x * (1.0 + sc_f) + sh_f
        y = jnp.square(jax.nn.relu(h @ w1[l])) @ w2[l]
        x = layer_norm(DEEPNORM_ALPHA * x + (1.0 + g_f) * y, ln_g[l, 1], ln_b[l, 1])
    return x
```

```python
import functools
import math

import jax
import jax.numpy as jnp
from jax import lax
from jax.experimental import pallas as pl
from jax.experimental.pallas import tpu as pltpu

F32 = jnp.float32
BF16 = jnp.bfloat16

D_MODEL = 2048
DEPTH = 4
RET_HEADS = 4
RET_DIM = 512
HEAD_DIM = 128
ROPE_BASE = 10000.0
HG_HEADS = 4
HG_DIM = 512
M_DIM = 1024
M_HEAD_DIM = 64
M_HEADS = 16
M_GROUPS = 2
M_STATE = 128
M_CONV = 4
M_CONV_DIM = 1536
D_FF = 4 * D_MODEL
IN_COLS = 6672
DEEPNORM_ALPHA = (2 * DEPTH) ** 0.25
LN_EPS = 1e-5
RMS_EPS = 1e-6
N_ADA = 6

LANES = 128
SUBLANES = 8
IN_COLS_PAD = 6912
DT_COL = 6656
VMEM_LIMIT = 56 * 2 ** 20

CHUNK = 128
HG_SUB = 8
HG_LEVELS = tuple(HG_SUB << i for i in range(int(math.log2(CHUNK // HG_SUB))))
N_HG_MATS = len(HG_LEVELS) + 3


def _params(*sem):
    return pltpu.CompilerParams(dimension_semantics=sem, vmem_limit_bytes=VMEM_LIMIT)


def _silu(x):
    return x * jax.nn.sigmoid(x)


def _dot(a, b):
    return jnp.dot(a, b, preferred_element_type=F32)


def _dot_nt(a, b):
    return lax.dot_general(a, b, (((1,), (1,)), ((), ())), preferred_element_type=F32)


def _dot_tn(a, b):
    return lax.dot_general(a, b, (((0,), (0,)), ((), ())), preferred_element_type=F32)


def _split3(x):
    hi = x.astype(BF16)
    r = x - hi.astype(F32)
    mid = r.astype(BF16)
    lo = (r - mid.astype(F32)).astype(BF16)
    return hi, mid, lo


def _dot_exact_rhs(m_bf16, x):
    hi, mid, lo = _split3(x)
    return _dot(m_bf16, hi) + _dot(m_bf16, mid) + _dot(m_bf16, lo)


def _dot_exact_lhs(x, m_bf16):
    hi, mid, lo = _split3(x)
    return _dot(hi, m_bf16) + _dot(mid, m_bf16) + _dot(lo, m_bf16)


def _layer_norm(v, g, b):
    mu = jnp.mean(v, axis=-1, keepdims=True)
    d = v - mu
    var = jnp.mean(d * d, axis=-1, keepdims=True)
    return d * lax.rsqrt(var + LN_EPS) * g + b


def _ada_kernel(c_ref, w_ref, b_ref, o_ref):
    cond = _silu(c_ref[...]).astype(BF16)
    o_ref[0] = _dot(cond, w_ref[0].astype(BF16)) + b_ref[0]


def _ada(c, w_ada, b_ada):
    bsz = c.shape[0]
    tn = 1024
    return pl.pallas_call(
        _ada_kernel,
        grid=(DEPTH, N_ADA * D_MODEL // tn),
        in_specs=[
            pl.BlockSpec((bsz, D_MODEL), lambda l, n: (0, 0)),
            pl.BlockSpec((1, D_MODEL, tn), lambda l, n: (l, 0, n)),
            pl.BlockSpec((1, 1, tn), lambda l, n: (l, 0, n)),
        ],
        out_specs=pl.BlockSpec((1, bsz, tn), lambda l, n: (l, 0, n)),
        out_shape=jax.ShapeDtypeStruct((DEPTH, bsz, N_ADA * D_MODEL), F32),
        compiler_params=_params("arbitrary", "arbitrary"),
        name="ada",
    )(c, w_ada, b_ada.reshape(DEPTH, 1, N_ADA * D_MODEL))


def _inproj_kernel(x_ref, ada_ref, w_ref, o_ref, h_ref):
    @pl.when(pl.program_id(2) == 0)
    def _():
        sh = ada_ref[0, 0:1, :]
        sc = ada_ref[0, 1:2, :]
        h_ref[...] = (x_ref[0] * (1.0 + sc) + sh).astype(BF16)

    o_ref[0] = _dot(h_ref[...], w_ref[...])


def _inproj(x, ada, w_in_p, layer):
    bsz, seq, _ = x.shape
    tm = min(512, seq)
    n_tiles = 3
    tn = IN_COLS_PAD // n_tiles
    return pl.pallas_call(
        _inproj_kernel,
        grid=(bsz, seq // tm, n_tiles),
        in_specs=[
            pl.BlockSpec((1, tm, D_MODEL), lambda b, m, n: (b, m, 0)),
            pl.BlockSpec((1, N_ADA, D_MODEL), lambda b, m, n: (layer * bsz + b, 0, 0)),
            pl.BlockSpec((None, D_MODEL, tn), lambda b, m, n: (layer, 0, n)),
        ],
        out_specs=pl.BlockSpec((1, tm, tn), lambda b, m, n: (b, m, n)),
        out_shape=jax.ShapeDtypeStruct((bsz, seq, IN_COLS_PAD), F32),
        scratch_shapes=[pltpu.VMEM((tm, D_MODEL), BF16)],
        compiler_params=_params("arbitrary", "arbitrary", "arbitrary"),
        name="inproj",
    )(x, ada, w_in_p)


def _ret_kernel(pos_ref, q_ref, k_ref, v_ref, g_ref, invf_ref, o_ref, s_ref):
    @pl.when(pl.program_id(1) == 0)
    def _():
        s_ref[...] = jnp.zeros_like(s_ref)

    c = CHUNK
    ang = pos_ref[0].astype(F32) * invf_ref[...]
    lane = lax.broadcasted_iota(jnp.int32, (1, HEAD_DIM), 1)
    cos2 = jnp.cos(ang)
    sin2 = jnp.where(lane < HEAD_DIM // 2, -1.0, 1.0) * jnp.sin(ang)
    row = lax.broadcasted_iota(jnp.int32, (c, 1), 0)
    col = lax.broadcasted_iota(jnp.int32, (1, c), 1)
    diff = row - col
    rowf = row.astype(F32)

    def rot(t):
        return t * cos2 + pltpu.roll(t, HEAD_DIM // 2, 1) * sin2

    for h in range(RET_HEADS):
        log_gamma = math.log(1.0 - 2.0 ** (-5.0 - h))
        sl = slice(h * HEAD_DIM, (h + 1) * HEAD_DIM)
        q = rot(q_ref[0, :, sl])
        k = rot(k_ref[0, :, sl]) * (HEAD_DIM ** -0.5)
        v = v_ref[0, :, sl].astype(BF16)
        decay = jnp.where(diff >= 0, jnp.exp(diff.astype(F32) * log_gamma), 0.0)
        scores = _dot_nt(q.astype(BF16), k.astype(BF16)) * decay
        st = s_ref[h]
        y = _dot(scores.astype(BF16), v)
        y = y + _dot((q * jnp.exp((rowf + 1.0) * log_gamma)).astype(BF16), st.astype(BF16))
        k_end = k * jnp.exp((c - 1.0 - rowf) * log_gamma)
        s_ref[h] = math.exp(c * log_gamma) * st + _dot_tn(k_end.astype(BF16), v)
        o = y * lax.rsqrt(jnp.mean(y * y, axis=-1, keepdims=True) + RMS_EPS)
        o_ref[0, :, sl] = (_silu(g_ref[0, :, sl]) * o).astype(BF16)


def _retention(proj, pos3, invf2):
    bsz, seq, _ = proj.shape
    c = CHUNK

    def col(j):
        return pl.BlockSpec((1, c, RET_DIM), lambda b, i, j=j: (b, i, j))

    return pl.pallas_call(
        _ret_kernel,
        grid=(bsz, seq // c),
        in_specs=[
            pl.BlockSpec((1, c, 1), lambda b, i: (b, i, 0)),
            col(0), col(1), col(2), col(3),
            pl.BlockSpec((1, HEAD_DIM), lambda b, i: (0, 0)),
        ],
        out_specs=pl.BlockSpec((1, c, RET_DIM), lambda b, i: (b, i, 0)),
        out_shape=jax.ShapeDtypeStruct((bsz, seq, RET_DIM), BF16),
        scratch_shapes=[pltpu.VMEM((RET_HEADS, HEAD_DIM, HEAD_DIM), F32)],
        compiler_params=_params("arbitrary", "arbitrary"),
        name="retention",
    )(pos3, proj, proj, proj, proj, invf2)


def _hg_matrices():
    c = CHUNK
    t = jnp.arange(c)[:, None]
    s = jnp.arange(c)[None, :]
    mats = []
    for b in HG_LEVELS:
        ref = (t // (2 * b)) * (2 * b) + b - 1
        upper = (t > ref) & (s > ref) & (s <= t)
        lower = (t <= ref) & (s > t) & (s <= ref)
        mats.append(upper | lower)
    mats.append((s <= t) & (s // HG_SUB == t // HG_SUB))
    mats.append(s <= t)
    mats.append(s > t)
    return jnp.concatenate(mats, axis=0).astype(BF16)


def _hg_kernel(layer, q_ref, f_ref, i_ref, g_ref, lbl_ref, nw_ref, mats_ref, ones_ref, o_ref, s_ref):
    @pl.when(pl.program_id(1) == 0)
    def _():
        s_ref[...] = jnp.zeros_like(s_ref)

    c = CHUNK
    lbl = lbl_ref[...]
    e = jnp.exp(lbl - jnp.max(lbl, axis=0, keepdims=True))
    p = e / jnp.sum(e, axis=0, keepdims=True)
    lb = jnp.zeros((1, HG_DIM), F32)
    for d in range(1, layer + 1):
        lb = lb + p[d:d + 1, :]

    q = q_ref[0]
    f = lb + (1.0 - lb) * jax.nn.sigmoid(f_ref[0])
    log_f = jnp.log(f)
    k = 1.0 - f
    v = i_ref[0]
    v_bf = v.astype(BF16)

    sums = _dot_exact_rhs(mats_ref[...], log_f)
    n_lv = len(HG_LEVELS)
    lc = sums[n_lv * c:(n_lv + 1) * c]
    cum = sums[(n_lv + 1) * c:(n_lv + 2) * c]
    tail = sums[(n_lv + 2) * c:(n_lv + 3) * c]

    row = lax.broadcasted_iota(jnp.int32, (c, 1), 0)
    col = lax.broadcasted_iota(jnp.int32, (1, c), 1)

    ql, kl, masks = [], [], []
    for lv, b in enumerate(HG_LEVELS):
        e_l = jnp.exp(sums[lv * c:(lv + 1) * c])
        ql.append((q * e_l).astype(BF16))
        kl.append((k * e_l).astype(BF16))
        sh = int(math.log2(b))
        masks.append(((row >> (sh + 1)) == (col >> (sh + 1))) & (((row >> sh) & 1) == 1) & (((col >> sh) & 1) == 0))

    q_in = (q * jnp.exp(cum)).astype(BF16)
    k_st = (k * jnp.exp(tail)).astype(BF16)
    e_tot = jnp.exp(cum[c - 1:c, :])

    ys = []
    for h in range(HG_HEADS):
        sl = slice(h * HEAD_DIM, (h + 1) * HEAD_DIM)
        scores = jnp.zeros((c, c), F32)
        for lv in range(n_lv):
            scores = scores + jnp.where(masks[lv], _dot_nt(ql[lv][:, sl], kl[lv][:, sl]), 0.0)
        st_t = s_ref[h]
        y = _dot(scores.astype(BF16), v_bf[:, sl]) + _dot_nt(q_in[:, sl], st_t.astype(BF16))
        s_ref[h] = st_t * e_tot[:, sl] + _dot_tn(v_bf[:, sl], k_st[:, sl])
        ys.append(y)
    y = jnp.concatenate(ys, axis=1)

    r_sub = row & (HG_SUB - 1)

    def bcast_row(x, j):
        x3 = x.reshape(c // HG_SUB, HG_SUB, HG_DIM)[:, j:j + 1, :]
        return jnp.broadcast_to(x3, (c // HG_SUB, HG_SUB, HG_DIM)).reshape(c, HG_DIM)

    prods = []
    for j in range(HG_SUB):
        rel = jnp.where(r_sub >= j, jnp.exp(jnp.minimum(lc - bcast_row(lc, j), 0.0)), 0.0)
        prods.append((q * bcast_row(k, j) * rel).astype(BF16))
    head_sums = _dot(jnp.concatenate(prods, axis=0), ones_ref[...])
    for j in range(HG_SUB):
        y = y + head_sums[j * c:(j + 1) * c] * bcast_row(v, j)

    outs = []
    for h in range(HG_HEADS):
        sl = slice(h * HEAD_DIM, (h + 1) * HEAD_DIM)
        yh = y[:, sl]
        outs.append(yh * lax.rsqrt(jnp.mean(yh * yh, axis=-1, keepdims=True) + RMS_EPS))
    o = jnp.concatenate(outs, axis=1) * nw_ref[0] * _silu(g_ref[0])
    o_ref[0] = o.astype(BF16)


def _hgrn2(proj, lb_logits, hg_norm_w, layer, mats, ones_bd):
    bsz, seq, _ = proj.shape
    c = CHUNK

    def col(j):
        return pl.BlockSpec((1, c, HG_DIM), lambda b, i, j=j: (b, i, j))

    return pl.pallas_call(
        functools.partial(_hg_kernel, layer),
        grid=(bsz, seq // c),
        in_specs=[
            col(4), col(5), col(6), col(7),
            pl.BlockSpec((DEPTH, HG_DIM), lambda b, i: (0, 0)),
            pl.BlockSpec((1, 1, HG_DIM), lambda b, i: (layer, 0, 0)),
            pl.BlockSpec((N_HG_MATS * c, c), lambda b, i: (0, 0)),
            pl.BlockSpec((HG_DIM, HG_DIM), lambda b, i: (0, 0)),
        ],
        out_specs=pl.BlockSpec((1, c, HG_DIM), lambda b, i: (b, i, 0)),
        out_shape=jax.ShapeDtypeStruct((bsz, seq, HG_DIM), BF16),
        scratch_shapes=[pltpu.VMEM((HG_HEADS, HEAD_DIM, HEAD_DIM), F32)],
        compiler_params=_params("arbitrary", "arbitrary"),
        name="hgrn2",
    )(proj, proj, proj, proj, lb_logits, hg_norm_w.reshape(DEPTH, 1, HG_DIM), mats, ones_bd)


def _mamba_kernel(z_ref, xs_ref, bc_ref, dt_ref, cwx_ref, cwb_ref, cbx_ref, cbb_ref, dtb_ref, alog_ref,
                  dskip_ref, nw_ref, tri_ref, expand_ref, o_ref, s_ref, ext_x_ref, ext_b_ref):
    c = CHUNK
    pad = SUBLANES

    @pl.when(pl.program_id(1) == 0)
    def _():
        s_ref[...] = jnp.zeros_like(s_ref)
        ext_x_ref[0:pad, :] = jnp.zeros((pad, M_DIM), F32)
        ext_b_ref[0:pad, :] = jnp.zeros((pad, M_CONV_DIM - M_DIM), F32)

    def conv_silu(ext_ref, u_ref, w_ref, b_ref):
        ext_ref[pad:pad + c, :] = u_ref[0]
        acc = b_ref[0]
        for j in range(M_CONV):
            acc = acc + w_ref[0, j:j + 1, :] * ext_ref[pad - (M_CONV - 1) + j:pad - (M_CONV - 1) + j + c, :]
        ext_ref[0:pad, :] = ext_ref[c:c + pad, :]
        return _silu(acc)

    xs = conv_silu(ext_x_ref, xs_ref, cwx_ref, cbx_ref)
    bc = conv_silu(ext_b_ref, bc_ref, cwb_ref, cbb_ref)

    dt = jax.nn.softplus(dt_ref[0] + dtb_ref[0])
    a = -jnp.exp(alog_ref[0])
    cum = _dot_exact_rhs(tri_ref[...], dt * a)
    cum_t = cum.T
    e_cum = jnp.exp(cum)
    w_state = dt * jnp.exp(cum[c - 1:c, :] - cum)
    spread = _dot_exact_lhs(jnp.concatenate([dt, e_cum, w_state], axis=0), expand_ref[...])
    dt_e, e_cum_e, w_state_e = spread[0:c], spread[c:2 * c], spread[2 * c:3 * c]
    x_dt = (xs * dt_e).astype(BF16)
    x_st = (xs * w_state_e).astype(BF16)

    row = lax.broadcasted_iota(jnp.int32, (c, 1), 0)
    col = lax.broadcasted_iota(jnp.int32, (1, c), 1)
    causal = row >= col
    lane = lax.broadcasted_iota(jnp.int32, (1, LANES), 1)
    gw = M_DIM // M_GROUPS
    heads_per_group = M_HEADS // M_GROUPS

    ys = []
    for g in range(M_GROUPS):
        b_g = bc[:, g * M_STATE:(g + 1) * M_STATE].astype(BF16)
        c_g = bc[:, (M_GROUPS + g) * M_STATE:(M_GROUPS + g + 1) * M_STATE].astype(BF16)
        gram = _dot_nt(c_g, b_g)
        st = s_ref[g]
        gsl = slice(g * gw, (g + 1) * gw)
        y_g = _dot(c_g, st.astype(BF16)) * e_cum_e[:, gsl]
        s_ref[g] = st * e_cum_e[c - 1:c, gsl] + _dot_tn(b_g, x_st[:, gsl])
        pieces = []
        for pr in range(heads_per_group // 2):
            psl = slice(g * gw + pr * LANES, g * gw + (pr + 1) * LANES)
            x_pair = x_dt[:, psl]
            acc = jnp.zeros((c, LANES), F32)
            for half in range(2):
                h = g * heads_per_group + 2 * pr + half
                seg = cum[:, h:h + 1] - cum_t[h:h + 1, :]
                decay = jnp.exp(jnp.where(causal, seg, -jnp.inf))
                in_head = (lane >= half * M_HEAD_DIM) & (lane < (half + 1) * M_HEAD_DIM)
                acc = acc + _dot((gram * decay).astype(BF16), jnp.where(in_head, x_pair, jnp.zeros_like(x_pair)))
            pieces.append(acc)
        ys.append(y_g + jnp.concatenate(pieces, axis=1))
    y = jnp.concatenate(ys, axis=1) + xs * dskip_ref[0]
    y = y * _silu(z_ref[0])
    outs = []
    for g in range(M_GROUPS):
        yg = y[:, g * gw:(g + 1) * gw]
        outs.append(yg * lax.rsqrt(jnp.mean(yg * yg, axis=-1, keepdims=True) + RMS_EPS))
    o_ref[0] = (jnp.concatenate(outs, axis=1) * nw_ref[0]).astype(BF16)


def _mamba2(proj, conv_w, conv_b3, dtb3, alog3, dskip3, norm_w3, layer, tri, expand):
    bsz, seq, _ = proj.shape
    c = CHUNK
    bcw = M_CONV_DIM - M_DIM

    def lay(shape, j=0):
        return pl.BlockSpec((1,) + shape, lambda b, i: (layer, 0, j))

    return pl.pallas_call(
        _mamba_kernel,
        grid=(bsz, seq // c),
        in_specs=[
            pl.BlockSpec((1, c, M_DIM), lambda b, i: (b, i, 4)),
            pl.BlockSpec((1, c, M_DIM), lambda b, i: (b, i, 5)),
            pl.BlockSpec((1, c, bcw), lambda b, i: (b, i, 12)),
            pl.BlockSpec((1, c, LANES), lambda b, i: (b, i, DT_COL // LANES)),
            lay((M_CONV, M_DIM)), lay((M_CONV, bcw), M_DIM // bcw),
            lay((1, M_DIM)), lay((1, bcw), M_DIM // bcw),
            lay((1, LANES)), lay((1, LANES)), lay((1, M_DIM)), lay((1, M_DIM)),
            pl.BlockSpec((c, c), lambda b, i: (0, 0)),
            pl.BlockSpec((LANES, M_DIM), lambda b, i: (0, 0)),
        ],
        out_specs=pl.BlockSpec((1, c, M_DIM), lambda b, i: (b, i, 0)),
        out_shape=jax.ShapeDtypeStruct((bsz, seq, M_DIM), BF16),
        scratch_shapes=[
            pltpu.VMEM((M_GROUPS, M_STATE, M_DIM // M_GROUPS), F32),
            pltpu.VMEM((SUBLANES + c, M_DIM), F32),
            pltpu.VMEM((SUBLANES + c, bcw), F32),
        ],
        compiler_params=_params("arbitrary", "arbitrary"),
        name="mamba2",
    )(proj, proj, proj, proj, conv_w, conv_w, conv_b3, conv_b3, dtb3, alog3, dskip3, norm_w3, tri, expand)


def _outproj_kernel(x_ref, oret_ref, ohg_ref, om_ref, w_ref, ada_ref, g_ref, b_ref, o_ref):
    y = _dot(oret_ref[0], w_ref[0:RET_DIM, :])
    y = y + _dot(ohg_ref[0], w_ref[RET_DIM:RET_DIM + HG_DIM, :])
    y = y + _dot(om_ref[0], w_ref[RET_DIM + HG_DIM:, :])
    gate = ada_ref[0, 2:3, :]
    o_ref[0] = _layer_norm(DEEPNORM_ALPHA * x_ref[0] + (1.0 + gate) * y, g_ref[0, 0:1, :], b_ref[0, 0:1, :])


def _outproj(x, o_ret, o_hg, o_m, w_out_b, ada, ln_g, ln_b, layer):
    bsz, seq, _ = x.shape
    tm = min(256, seq)

    def rows(width):
        return pl.BlockSpec((1, tm, width), lambda b, m: (b, m, 0))

    return pl.pallas_call(
        _outproj_kernel,
        grid=(bsz, seq // tm),
        in_specs=[
            rows(D_MODEL), rows(RET_DIM), rows(HG_DIM), rows(M_DIM),
            pl.BlockSpec((None, D_MODEL, D_MODEL), lambda b, m: (layer, 0, 0)),
            pl.BlockSpec((1, N_ADA, D_MODEL), lambda b, m: (layer * bsz + b, 0, 0)),
            pl.BlockSpec((1, 2, D_MODEL), lambda b, m: (layer, 0, 0)),
            pl.BlockSpec((1, 2, D_MODEL), lambda b, m: (layer, 0, 0)),
        ],
        out_specs=rows(D_MODEL),
        out_shape=jax.ShapeDtypeStruct(x.shape, F32),
        compiler_params=_params("arbitrary", "arbitrary"),
        name="outproj",
    )(x, o_ret, o_hg, o_m, w_out_b, ada, ln_g, ln_b)


def _mlp_kernel(x_ref, ada_ref, w1_ref, w2_ref, g_ref, b_ref, o_ref, h_ref, acc_ref):
    f = pl.program_id(2)

    @pl.when(f == 0)
    def _():
        sh = ada_ref[0, 3:4, :]
        sc = ada_ref[0, 4:5, :]
        h_ref[...] = (x_ref[0] * (1.0 + sc) + sh).astype(BF16)
        acc_ref[...] = jnp.zeros_like(acc_ref)

    u = jnp.maximum(_dot(h_ref[...], w1_ref[...]), 0.0)
    acc_ref[...] += _dot((u * u).astype(BF16), w2_ref[...])

    @pl.when(f == pl.num_programs(2) - 1)
    def _():
        gate = ada_ref[0, 5:6, :]
        o_ref[0] = _layer_norm(DEEPNORM_ALPHA * x_ref[0] + (1.0 + gate) * acc_ref[...],
                               g_ref[0, 1:2, :], b_ref[0, 1:2, :])


def _mlp(x, ada, w1_b, w2_b, ln_g, ln_b, layer):
    bsz, seq, _ = x.shape
    tm = min(512, seq)
    tf = 512
    return pl.pallas_call(
        _mlp_kernel,
        grid=(bsz, seq // tm, D_FF // tf),
        in_specs=[
            pl.BlockSpec((1, tm, D_MODEL), lambda b, m, f: (b, m, 0)),
            pl.BlockSpec((1, N_ADA, D_MODEL), lambda b, m, f: (layer * bsz + b, 0, 0)),
            pl.BlockSpec((None, D_MODEL, tf), lambda b, m, f: (layer, 0, f)),
            pl.BlockSpec((None, tf, D_MODEL), lambda b, m, f: (layer, f, 0)),
            pl.BlockSpec((1, 2, D_MODEL), lambda b, m, f: (layer, 0, 0)),
            pl.BlockSpec((1, 2, D_MODEL), lambda b, m, f: (layer, 0, 0)),
        ],
        out_specs=pl.BlockSpec((1, tm, D_MODEL), lambda b, m, f: (b, m, 0)),
        out_shape=jax.ShapeDtypeStruct(x.shape, F32),
        scratch_shapes=[pltpu.VMEM((tm, D_MODEL), BF16), pltpu.VMEM((tm, D_MODEL), F32)],
        compiler_params=_params("arbitrary", "arbitrary", "arbitrary"),
        name="mlp",
    )(x, ada, w1_b, w2_b, ln_g, ln_b)


def kernel(x, c, positions, lb_logits, w_in, w_out, w_ada, b_ada, ln_g, ln_b, hg_norm_w, m_conv_w, m_conv_b,
           m_dt_bias, m_a_log, m_d, m_norm_w, w1, w2):
    bsz, seq, _ = x.shape
    assert seq % CHUNK == 0

    w_in_p = jnp.pad(w_in, ((0, 0), (0, 0), (0, IN_COLS_PAD - IN_COLS))).astype(BF16)
    w_out_b = w_out.astype(BF16)
    w1_b = w1.astype(BF16)
    w2_b = w2.astype(BF16)
    pad_heads = ((0, 0), (0, LANES - M_HEADS))
    dtb3 = jnp.pad(m_dt_bias, pad_heads).reshape(DEPTH, 1, LANES)
    alog3 = jnp.pad(m_a_log, pad_heads).reshape(DEPTH, 1, LANES)
    dskip3 = jnp.repeat(m_d, M_HEAD_DIM, axis=1).reshape(DEPTH, 1, M_DIM)
    conv_b3 = m_conv_b.reshape(DEPTH, 1, M_CONV_DIM)
    norm_w3 = m_norm_w.reshape(DEPTH, 1, M_DIM)
    pos3 = positions.reshape(bsz, seq, 1)

    inv_freq = 1.0 / (ROPE_BASE ** jnp.linspace(0.0, 1.0, HEAD_DIM // 2, dtype=F32))
    invf2 = jnp.concatenate([inv_freq, inv_freq]).reshape(1, HEAD_DIM)
    idx = jnp.arange(CHUNK)
    tri = (idx[None, :] <= idx[:, None]).astype(BF16)
    expand = (jnp.arange(M_DIM)[None, :] // M_HEAD_DIM == jnp.arange(LANES)[:, None]).astype(BF16)
    hidx = jnp.arange(HG_DIM) // HEAD_DIM
    ones_bd = (hidx[:, None] == hidx[None, :]).astype(BF16)
    hg_mats = _hg_matrices()

    ada = _ada(c, w_ada, b_ada).reshape(DEPTH * bsz, N_ADA, D_MODEL)
    for l in range(DEPTH):
        proj = _inproj(x, ada, w_in_p, l)
        o_ret = _retention(proj, pos3, invf2)
        o_hg = _hgrn2(proj, lb_logits, hg_norm_w, l, hg_mats, ones_bd)
        o_m = _mamba2(proj, m_conv_w, conv_b3, dtb3, alog3, dskip3, norm_w3, l, tri, expand)
        x = _outproj(x, o_ret, o_hg, o_m, w_out_b, ada, ln_g, ln_b, l)
        x = _mlp(x, ada, w1_b, w2_b, ln_g, ln_b, l)
    return x
```

```python
import functools
import math

import jax
import jax.numpy as jnp
from jax import lax
from jax.experimental import pallas as pl
from jax.experimental.pallas import tpu as pltpu

F32 = jnp.float32
BF16 = jnp.bfloat16

D_MODEL = 2048
DEPTH = 4
RET_HEADS = 4
RET_DIM = 512
HEAD_DIM = 128
ROPE_BASE = 10000.0
HG_HEADS = 4
HG_DIM = 512
M_DIM = 1024
M_HEAD_DIM = 64
M_HEADS = 16
M_GROUPS = 2
M_STATE = 128
M_CONV = 4
M_CONV_DIM = 1536
D_FF = 4 * D_MODEL
IN_COLS = 6672
DEEPNORM_ALPHA = (2 * DEPTH) ** 0.25
LN_EPS = 1e-5
RMS_EPS = 1e-6
N_ADA = 6

LANES = 128
SUBLANES = 8
IN_COLS_PAD = 6912
DT_COL = 6656
VMEM_LIMIT = 56 * 2 ** 20

ADA_TN = 1024
ROPE_TM = 512
INPROJ_TM, INPROJ_TN = 1024, 768
OUTPROJ_TM = 512
MLP_TM, MLP_TF = 1024, 512
MLP_TN = 512
LN_ROWS = 128

CHUNK = 128
HG_LEVELS = tuple(1 << i for i in range(int(math.log2(CHUNK))))


def _params(*sem):
    return pltpu.CompilerParams(dimension_semantics=sem, vmem_limit_bytes=VMEM_LIMIT)


def _silu(x):
    return x * jax.nn.sigmoid(x)


def _dot(a, b):
    return jnp.dot(a, b, preferred_element_type=F32)


def _dot_nt(a, b):
    return lax.dot_general(a, b, (((1,), (1,)), ((), ())), preferred_element_type=F32)


def _dot_tn(a, b):
    return lax.dot_general(a, b, (((0,), (0,)), ((), ())), preferred_element_type=F32)


def _split3(x):
    hi = x.astype(BF16)
    r = x - hi.astype(F32)
    mid = r.astype(BF16)
    lo = (r - mid.astype(F32)).astype(BF16)
    return hi, mid, lo


def _dot_exact_rhs(m_bf16, x):
    hi, mid, lo = _split3(x)
    return _dot(m_bf16, hi) + _dot(m_bf16, mid) + _dot(m_bf16, lo)


def _dot_exact_lhs(x, m_bf16):
    hi, mid, lo = _split3(x)
    return _dot(hi, m_bf16) + _dot(mid, m_bf16) + _dot(lo, m_bf16)


def _layer_norm(v, g, b):
    mu = jnp.mean(v, axis=-1, keepdims=True)
    d = v - mu
    var = jnp.mean(d * d, axis=-1, keepdims=True)
    return d * lax.rsqrt(var + LN_EPS) * g + b


def _slab_spec(w, layer, n_chunks, n_steps):
    rows = w.shape[1] // n_steps
    assert rows * n_steps == w.shape[1] and rows % (2 * SUBLANES) == 0
    return pl.BlockSpec((None, rows, w.shape[2]), lambda b, i: (layer, b * n_chunks + i, 0))


def _slab_out(w, n_chunks, n_steps, cols=None):
    rows = w.shape[1] // n_steps
    cols = w.shape[2] if cols is None else cols
    return (pl.BlockSpec((rows, cols), lambda b, i: (b * n_chunks + i, 0)),
            jax.ShapeDtypeStruct((w.shape[1], cols), BF16))


def _ada_kernel(c_ref, w_ref, b_ref, o_ref):
    cond = _silu(c_ref[...]).astype(BF16)
    o_ref[0] = _dot(cond, w_ref[0].astype(BF16)) + b_ref[0]


def _ada(c, w_ada, b_ada):
    bsz = c.shape[0]
    tn = ADA_TN
    return pl.pallas_call(
        _ada_kernel,
        grid=(DEPTH, N_ADA * D_MODEL // tn),
        in_specs=[
            pl.BlockSpec((bsz, D_MODEL), lambda l, n: (0, 0)),
            pl.BlockSpec((1, D_MODEL, tn), lambda l, n: (l, 0, n)),
            pl.BlockSpec((1, 1, tn), lambda l, n: (l, 0, n)),
        ],
        out_specs=pl.BlockSpec((1, bsz, tn), lambda l, n: (l, 0, n)),
        out_shape=jax.ShapeDtypeStruct((DEPTH, bsz, N_ADA * D_MODEL), F32),
        compiler_params=_params("arbitrary", "arbitrary"),
        name="ada",
    )(c, w_ada, b_ada.reshape(DEPTH, 1, N_ADA * D_MODEL))


def _rope_kernel(pos_ref, invf_ref, cos_ref, sin_ref):
    ang = pos_ref[0].astype(F32) * invf_ref[...]
    lane = lax.broadcasted_iota(jnp.int32, (1, HEAD_DIM), 1)
    cos_ref[0] = jnp.cos(ang)
    sin_ref[0] = jnp.where(lane < HEAD_DIM // 2, -1.0, 1.0) * jnp.sin(ang)


def _rope(pos3, invf2):
    bsz, seq, _ = pos3.shape
    tm = min(ROPE_TM, seq)
    tab = pl.BlockSpec((1, tm, HEAD_DIM), lambda b, m: (b, m, 0))
    shape = jax.ShapeDtypeStruct((bsz, seq, HEAD_DIM), F32)
    return pl.pallas_call(
        _rope_kernel,
        grid=(bsz, seq // tm),
        in_specs=[pl.BlockSpec((1, tm, 1), lambda b, m: (b, m, 0)),
                  pl.BlockSpec((1, HEAD_DIM), lambda b, m: (0, 0))],
        out_specs=(tab, tab),
        out_shape=(shape, shape),
        compiler_params=_params("arbitrary", "arbitrary"),
        name="rope",
    )(pos3, invf2)


def _inproj_kernel(x_ref, ada_ref, w_ref, o_ref, h_ref):
    @pl.when(pl.program_id(2) == 0)
    def _():
        sh = ada_ref[0, 0:1, :]
        sc = ada_ref[0, 1:2, :]
        h_ref[...] = (x_ref[0] * (1.0 + sc) + sh).astype(BF16)

    o_ref[0] = _dot(h_ref[...], w_ref[...])


def _inproj(x, ada, w_in_b, layer):
    bsz, seq, _ = x.shape
    tm = min(INPROJ_TM, seq)
    tn = INPROJ_TN
    return pl.pallas_call(
        _inproj_kernel,
        grid=(bsz, seq // tm, IN_COLS_PAD // tn),
        in_specs=[
            pl.BlockSpec((1, tm, D_MODEL), lambda b, m, n: (b, m, 0)),
            pl.BlockSpec((1, N_ADA, D_MODEL), lambda b, m, n: (layer * bsz + b, 0, 0)),
            pl.BlockSpec((D_MODEL, tn), lambda b, m, n: (0, n)),
        ],
        out_specs=pl.BlockSpec((1, tm, tn), lambda b, m, n: (b, m, n)),
        out_shape=jax.ShapeDtypeStruct((bsz, seq, IN_COLS_PAD), F32),
        scratch_shapes=[pltpu.VMEM((tm, D_MODEL), BF16)],
        compiler_params=_params("arbitrary", "arbitrary", "arbitrary"),
        name="inproj",
    )(x, ada, w_in_b)


def _ret_kernel(has_next, *refs):
    if has_next:
        (cos_ref, sin_ref, q_ref, k_ref, v_ref, g_ref, wout_ref, win_ref,
         o_ref, wout_o_ref, win_o_ref, s_ref) = refs
        win_o_ref[:, 0:IN_COLS] = win_ref[...].astype(BF16)
        win_o_ref[:, IN_COLS:] = jnp.zeros((win_o_ref.shape[0], IN_COLS_PAD - IN_COLS), BF16)
    else:
        cos_ref, sin_ref, q_ref, k_ref, v_ref, g_ref, wout_ref, o_ref, wout_o_ref, s_ref = refs
    wout_o_ref[...] = wout_ref[...].astype(BF16)

    @pl.when(pl.program_id(1) == 0)
    def _():
        s_ref[...] = jnp.zeros_like(s_ref)

    c = CHUNK
    cos2 = cos_ref[0]
    sin2 = sin_ref[0]
    row = lax.broadcasted_iota(jnp.int32, (c, 1), 0)
    col = lax.broadcasted_iota(jnp.int32, (1, c), 1)
    diff = row - col
    rowf = row.astype(F32)

    def rot(t):
        return t * cos2 + pltpu.roll(t, HEAD_DIM // 2, 1) * sin2

    for h in range(RET_HEADS):
        log_gamma = math.log(1.0 - 2.0 ** (-5.0 - h))
        sl = slice(h * HEAD_DIM, (h + 1) * HEAD_DIM)
        q = rot(q_ref[0, :, sl])
        k = rot(k_ref[0, :, sl]) * (HEAD_DIM ** -0.5)
        v = v_ref[0, :, sl].astype(BF16)
        decay = jnp.where(diff >= 0, jnp.exp(diff.astype(F32) * log_gamma), 0.0)
        scores = _dot_nt(q.astype(BF16), k.astype(BF16)) * decay
        st = s_ref[h]
        y = _dot(scores.astype(BF16), v)
        y = y + _dot((q * jnp.exp((rowf + 1.0) * log_gamma)).astype(BF16), st.astype(BF16))
        k_end = k * jnp.exp((c - 1.0 - rowf) * log_gamma)
        s_ref[h] = math.exp(c * log_gamma) * st + _dot_tn(k_end.astype(BF16), v)
        o = y * lax.rsqrt(jnp.mean(y * y, axis=-1, keepdims=True) + RMS_EPS)
        o_ref[0, :, sl] = (_silu(g_ref[0, :, sl]) * o).astype(BF16)


def _retention(proj, cos2, sin2, w_out, w_in, layer):
    bsz, seq, _ = proj.shape
    c = CHUNK
    n_chunks = seq // c
    n_steps = bsz * n_chunks
    has_next = layer + 1 < DEPTH

    def col(j):
        return pl.BlockSpec((1, c, RET_DIM), lambda b, i, j=j: (b, i, j))

    tab = pl.BlockSpec((1, c, HEAD_DIM), lambda b, i: (b, i, 0))
    in_specs = [tab, tab, col(0), col(1), col(2), col(3), _slab_spec(w_out, layer, n_chunks, n_steps)]
    args = [cos2, sin2, proj, proj, proj, proj, w_out]
    out_specs = [pl.BlockSpec((1, c, RET_DIM), lambda b, i: (b, i, 0))]
    out_shape = [jax.ShapeDtypeStruct((bsz, seq, RET_DIM), BF16)]
    spec, shape = _slab_out(w_out, n_chunks, n_steps)
    out_specs.append(spec)
    out_shape.append(shape)
    if has_next:
        in_specs.append(_slab_spec(w_in, layer + 1, n_chunks, n_steps))
        args.append(w_in)
        spec, shape = _slab_out(w_in, n_chunks, n_steps, IN_COLS_PAD)
        out_specs.append(spec)
        out_shape.append(shape)
    return pl.pallas_call(
        functools.partial(_ret_kernel, has_next),
        grid=(bsz, n_chunks),
        in_specs=in_specs,
        out_specs=out_specs,
        out_shape=out_shape,
        scratch_shapes=[pltpu.VMEM((RET_HEADS, HEAD_DIM, HEAD_DIM), F32)],
        compiler_params=_params("arbitrary", "arbitrary"),
        name="retention",
    )(*args)


def _hg_kernel(layer, q_ref, f_ref, i_ref, g_ref, lbl_ref, nw_ref, tri_ref, w_ref, o_ref, w_o_ref, s_ref):
    w_o_ref[...] = w_ref[...].astype(BF16)

    @pl.when(pl.program_id(1) == 0)
    def _():
        s_ref[...] = jnp.zeros_like(s_ref)

    c = CHUNK
    lbl = lbl_ref[...]
    e = jnp.exp(lbl - jnp.max(lbl, axis=0, keepdims=True))
    p = e / jnp.sum(e, axis=0, keepdims=True)
    lb = jnp.zeros((1, HG_DIM), F32)
    for d in range(1, layer + 1):
        lb = lb + p[d:d + 1, :]

    q = q_ref[0]
    f = lb + (1.0 - lb) * jax.nn.sigmoid(f_ref[0])
    log_f = jnp.log(f)
    k = 1.0 - f
    v_bf = i_ref[0].astype(BF16)
    q_bf = q.astype(BF16)
    k_bf = k.astype(BF16)

    cum = _dot_exact_rhs(tri_ref[...], log_f)
    row = lax.broadcasted_iota(jnp.int32, (c, 1), 0)
    col = lax.broadcasted_iota(jnp.int32, (1, c), 1)
    r_sub = row & (SUBLANES - 1)

    def pick_row(x, block, j):
        x3 = x.reshape(c // block, block, HG_DIM)[:, j:j + 1, :]
        return jnp.broadcast_to(x3, (c // block, block, HG_DIM)).reshape(c, HG_DIM)

    merged, masks = [], []
    for b in HG_LEVELS:
        sh = int(math.log2(b))
        upper = ((row >> sh) & 1) == 1
        if b == 1:
            m = jnp.where(upper, q * f, k)
        else:
            if 2 * b >= SUBLANES:
                ref = pick_row(cum, 2 * b, b - 1)
            else:
                ref = jnp.where(r_sub < 2 * b, pick_row(cum, SUBLANES, b - 1), pick_row(cum, SUBLANES, 3 * b - 1))
            m = jnp.where(upper, q, k) * jnp.exp(-jnp.abs(cum - ref))
        merged.append(m.astype(BF16))
        masks.append(((row >> (sh + 1)) == (col >> (sh + 1))) & upper & (((col >> sh) & 1) == 0))

    tail = cum[c - 1:c, :] - cum
    q_in = (q * jnp.exp(cum)).astype(BF16)
    k_st = (k * jnp.exp(tail)).astype(BF16)
    e_tot = jnp.exp(cum[c - 1:c, :])

    outs = []
    for h in range(HG_HEADS):
        sl = slice(h * HEAD_DIM, (h + 1) * HEAD_DIM)
        scores = jnp.where(row == col, _dot_nt(q_bf[:, sl], k_bf[:, sl]), 0.0)
        for lv in range(len(HG_LEVELS)):
            m_h = merged[lv][:, sl]
            scores = scores + jnp.where(masks[lv], _dot_nt(m_h, m_h), 0.0)
        st_t = s_ref[h]
        y = _dot(scores.astype(BF16), v_bf[:, sl]) + _dot_nt(q_in[:, sl], st_t.astype(BF16))
        s_ref[h] = st_t * e_tot[:, sl] + _dot_tn(v_bf[:, sl], k_st[:, sl])
        outs.append(y * lax.rsqrt(jnp.mean(y * y, axis=-1, keepdims=True) + RMS_EPS))
    o = jnp.concatenate(outs, axis=1) * nw_ref[0] * _silu(g_ref[0])
    o_ref[0] = o.astype(BF16)


def _hgrn2(proj, lb_logits, hg_norm_w, tri, w1, layer):
    bsz, seq, _ = proj.shape
    c = CHUNK
    n_chunks = seq // c
    n_steps = bsz * n_chunks

    def col(j):
        return pl.BlockSpec((1, c, HG_DIM), lambda b, i, j=j: (b, i, j))

    w_spec, w_shape = _slab_out(w1, n_chunks, n_steps)
    return pl.pallas_call(
        functools.partial(_hg_kernel, layer),
        grid=(bsz, n_chunks),
        in_specs=[
            col(4), col(5), col(6), col(7),
            pl.BlockSpec((DEPTH, HG_DIM), lambda b, i: (0, 0)),
            pl.BlockSpec((1, 1, HG_DIM), lambda b, i: (layer, 0, 0)),
            pl.BlockSpec((c, c), lambda b, i: (0, 0)),
            _slab_spec(w1, layer, n_chunks, n_steps),
        ],
        out_specs=[pl.BlockSpec((1, c, HG_DIM), lambda b, i: (b, i, 0)), w_spec],
        out_shape=[jax.ShapeDtypeStruct((bsz, seq, HG_DIM), BF16), w_shape],
        scratch_shapes=[pltpu.VMEM((HG_HEADS, HEAD_DIM, HEAD_DIM), F32)],
        compiler_params=_params("arbitrary", "arbitrary"),
        name="hgrn2",
    )(proj, proj, proj, proj, lb_logits, hg_norm_w.reshape(DEPTH, 1, HG_DIM), tri, w1)


def _mamba_kernel(z_ref, xs_ref, bc_ref, dt_ref, cwx_ref, cwb_ref, cbx_ref, cbb_ref, dtb_ref, alog_ref,
                  dskip_ref, nw_ref, tri_ref, expand_ref, w_ref, o_ref, w_o_ref, s_ref, ext_x_ref, ext_b_ref):
    w_o_ref[...] = w_ref[...].astype(BF16)
    c = CHUNK
    pad = SUBLANES

    @pl.when(pl.program_id(1) == 0)
    def _():
        s_ref[...] = jnp.zeros_like(s_ref)
        ext_x_ref[0:pad, :] = jnp.zeros((pad, M_DIM), F32)
        ext_b_ref[0:pad, :] = jnp.zeros((pad, M_CONV_DIM - M_DIM), F32)

    def conv_silu(ext_ref, u_ref, w_ref, b_ref):
        ext_ref[pad:pad + c, :] = u_ref[0]
        acc = b_ref[0]
        for j in range(M_CONV):
            acc = acc + w_ref[0, j:j + 1, :] * ext_ref[pad - (M_CONV - 1) + j:pad - (M_CONV - 1) + j + c, :]
        ext_ref[0:pad, :] = ext_ref[c:c + pad, :]
        return _silu(acc)

    xs = conv_silu(ext_x_ref, xs_ref, cwx_ref, cbx_ref)
    bc = conv_silu(ext_b_ref, bc_ref, cwb_ref, cbb_ref)

    dt = jax.nn.softplus(dt_ref[0] + dtb_ref[0])
    a = -jnp.exp(alog_ref[0])
    cum = _dot_exact_rhs(tri_ref[...], dt * a)
    cum_t = cum.T
    e_cum = jnp.exp(cum)
    w_state = dt * jnp.exp(cum[c - 1:c, :] - cum)
    spread = _dot_exact_lhs(jnp.concatenate([dt, e_cum, w_state], axis=0), expand_ref[...])
    dt_e, e_cum_e, w_state_e = spread[0:c], spread[c:2 * c], spread[2 * c:3 * c]
    x_dt = (xs * dt_e).astype(BF16)
    x_st = (xs * w_state_e).astype(BF16)

    row = lax.broadcasted_iota(jnp.int32, (c, 1), 0)
    col = lax.broadcasted_iota(jnp.int32, (1, c), 1)
    causal = row >= col
    lane = lax.broadcasted_iota(jnp.int32, (1, LANES), 1)
    gw = M_DIM // M_GROUPS
    heads_per_group = M_HEADS // M_GROUPS

    ys = []
    for g in range(M_GROUPS):
        b_g = bc[:, g * M_STATE:(g + 1) * M_STATE].astype(BF16)
        c_g = bc[:, (M_GROUPS + g) * M_STATE:(M_GROUPS + g + 1) * M_STATE].astype(BF16)
        gram = _dot_nt(c_g, b_g)
        st = s_ref[g]
        gsl = slice(g * gw, (g + 1) * gw)
        y_g = _dot(c_g, st.astype(BF16)) * e_cum_e[:, gsl]
        s_ref[g] = st * e_cum_e[c - 1:c, gsl] + _dot_tn(b_g, x_st[:, gsl])
        pieces = []
        for pr in range(heads_per_group // 2):
            psl = slice(g * gw + pr * LANES, g * gw + (pr + 1) * LANES)
            x_pair = x_dt[:, psl]
            acc = jnp.zeros((c, LANES), F32)
            for half in range(2):
                h = g * heads_per_group + 2 * pr + half
                seg = cum[:, h:h + 1] - cum_t[h:h + 1, :]
                decay = jnp.exp(jnp.where(causal, seg, -jnp.inf))
                in_head = (lane >= half * M_HEAD_DIM) & (lane < (half + 1) * M_HEAD_DIM)
                acc = acc + _dot((gram * decay).astype(BF16), jnp.where(in_head, x_pair, jnp.zeros_like(x_pair)))
            pieces.append(acc)
        ys.append(y_g + jnp.concatenate(pieces, axis=1))
    y = jnp.concatenate(ys, axis=1) + xs * dskip_ref[0]
    y = y * _silu(z_ref[0])
    outs = []
    for g in range(M_GROUPS):
        yg = y[:, g * gw:(g + 1) * gw]
        outs.append(yg * lax.rsqrt(jnp.mean(yg * yg, axis=-1, keepdims=True) + RMS_EPS))
    o_ref[0] = (jnp.concatenate(outs, axis=1) * nw_ref[0]).astype(BF16)


def _mamba2(proj, conv_w, conv_b3, dtb3, alog3, dskip3, norm_w3, tri, expand, w2, layer):
    bsz, seq, _ = proj.shape
    c = CHUNK
    n_chunks = seq // c
    n_steps = bsz * n_chunks
    bcw = M_CONV_DIM - M_DIM

    def lay(shape, j=0):
        return pl.BlockSpec((1,) + shape, lambda b, i: (layer, 0, j))

    w_spec, w_shape = _slab_out(w2, n_chunks, n_steps)
    return pl.pallas_call(
        _mamba_kernel,
        grid=(bsz, n_chunks),
        in_specs=[
            pl.BlockSpec((1, c, M_DIM), lambda b, i: (b, i, 4)),
            pl.BlockSpec((1, c, M_DIM), lambda b, i: (b, i, 5)),
            pl.BlockSpec((1, c, bcw), lambda b, i: (b, i, 12)),
            pl.BlockSpec((1, c, LANES), lambda b, i: (b, i, DT_COL // LANES)),
            lay((M_CONV, M_DIM)), lay((M_CONV, bcw), M_DIM // bcw),
            lay((1, M_DIM)), lay((1, bcw), M_DIM // bcw),
            lay((1, LANES)), lay((1, LANES)), lay((1, M_DIM)), lay((1, M_DIM)),
            pl.BlockSpec((c, c), lambda b, i: (0, 0)),
            pl.BlockSpec((LANES, M_DIM), lambda b, i: (0, 0)),
            _slab_spec(w2, layer, n_chunks, n_steps),
        ],
        out_specs=[pl.BlockSpec((1, c, M_DIM), lambda b, i: (b, i, 0)), w_spec],
        out_shape=[jax.ShapeDtypeStruct((bsz, seq, M_DIM), BF16), w_shape],
        scratch_shapes=[
            pltpu.VMEM((M_GROUPS, M_STATE, M_DIM // M_GROUPS), F32),
            pltpu.VMEM((SUBLANES + c, M_DIM), F32),
            pltpu.VMEM((SUBLANES + c, bcw), F32),
        ],
        compiler_params=_params("arbitrary", "arbitrary"),
        name="mamba2",
    )(proj, proj, proj, proj, conv_w, conv_w, conv_b3, conv_b3, dtb3, alog3, dskip3, norm_w3, tri, expand, w2)


def _residual_norm(x_ref, y_ref, o_ref, gate, g, b, rows):
    for r in range(0, rows, LN_ROWS):
        rs = slice(r, r + LN_ROWS)
        o_ref[0, rs, :] = _layer_norm(DEEPNORM_ALPHA * x_ref[0, rs, :] + (1.0 + gate) * y_ref[0, rs, :], g, b)


def _outproj_kernel(x_ref, oret_ref, ohg_ref, om_ref, w_ref, ada_ref, g_ref, b_ref, o_ref, cat_ref):
    cat_ref[:, 0:RET_DIM] = oret_ref[0]
    cat_ref[:, RET_DIM:RET_DIM + HG_DIM] = ohg_ref[0]
    cat_ref[:, RET_DIM + HG_DIM:] = om_ref[0]
    o_ref[0] = _dot(cat_ref[...], w_ref[...])
    _residual_norm(x_ref, o_ref, o_ref, ada_ref[0, 2:3, :], g_ref[0, 0:1, :], b_ref[0, 0:1, :], o_ref.shape[1])


def _outproj(x, o_ret, o_hg, o_m, w_out_b, ada, ln_g, ln_b, layer):
    bsz, seq, _ = x.shape
    tm = min(OUTPROJ_TM, seq)

    def rows(width):
        return pl.BlockSpec((1, tm, width), lambda b, m: (b, m, 0))

    return pl.pallas_call(
        _outproj_kernel,
        grid=(bsz, seq // tm),
        in_specs=[
            rows(D_MODEL), rows(RET_DIM), rows(HG_DIM), rows(M_DIM),
            pl.BlockSpec((D_MODEL, D_MODEL), lambda b, m: (0, 0)),
            pl.BlockSpec((1, N_ADA, D_MODEL), lambda b, m: (layer * bsz + b, 0, 0)),
            pl.BlockSpec((1, 2, D_MODEL), lambda b, m: (layer, 0, 0)),
            pl.BlockSpec((1, 2, D_MODEL), lambda b, m: (layer, 0, 0)),
        ],
        out_specs=rows(D_MODEL),
        out_shape=jax.ShapeDtypeStruct(x.shape, F32),
        scratch_shapes=[pltpu.VMEM((tm, D_MODEL), BF16)],
        compiler_params=_params("arbitrary", "arbitrary"),
        name="outproj",
    )(x, o_ret, o_hg, o_m, w_out_b, ada, ln_g, ln_b)


def _mlp_kernel(x_ref, ada_ref, w1_ref, w2_ref, g_ref, b_ref, o_ref, h_ref):
    f = pl.program_id(2)

    @pl.when(f == 0)
    def _():
        sh = ada_ref[0, 3:4, :]
        sc = ada_ref[0, 4:5, :]
        h_ref[...] = (x_ref[0] * (1.0 + sc) + sh).astype(BF16)

    u = jnp.maximum(_dot(h_ref[...], w1_ref[...]), 0.0)
    u2 = (u * u).astype(BF16)

    @pl.when(f == 0)
    def _():
        for n in range(0, D_MODEL, MLP_TN):
            o_ref[0, :, n:n + MLP_TN] = _dot(u2, w2_ref[:, n:n + MLP_TN])

    @pl.when(f > 0)
    def _():
        for n in range(0, D_MODEL, MLP_TN):
            o_ref[0, :, n:n + MLP_TN] += _dot(u2, w2_ref[:, n:n + MLP_TN])

    @pl.when(f == pl.num_programs(2) - 1)
    def _():
        _residual_norm(x_ref, o_ref, o_ref, ada_ref[0, 5:6, :], g_ref[0, 1:2, :], b_ref[0, 1:2, :], o_ref.shape[1])


def _mlp(x, ada, w1_b, w2_b, ln_g, ln_b, layer):
    bsz, seq, _ = x.shape
    tm = min(MLP_TM, seq)
    tf = MLP_TF
    return pl.pallas_call(
        _mlp_kernel,
        grid=(bsz, seq // tm, D_FF // tf),
        in_specs=[
            pl.BlockSpec((1, tm, D_MODEL), lambda b, m, f: (b, m, 0)),
            pl.BlockSpec((1, N_ADA, D_MODEL), lambda b, m, f: (layer * bsz + b, 0, 0)),
            pl.BlockSpec((D_MODEL, tf), lambda b, m, f: (0, f)),
            pl.BlockSpec((tf, D_MODEL), lambda b, m, f: (f, 0)),
            pl.BlockSpec((1, 2, D_MODEL), lambda b, m, f: (layer, 0, 0)),
            pl.BlockSpec((1, 2, D_MODEL), lambda b, m, f: (layer, 0, 0)),
        ],
        out_specs=pl.BlockSpec((1, tm, D_MODEL), lambda b, m, f: (b, m, 0)),
        out_shape=jax.ShapeDtypeStruct(x.shape, F32),
        scratch_shapes=[pltpu.VMEM((tm, D_MODEL), BF16)],
        compiler_params=_params("arbitrary", "arbitrary", "arbitrary"),
        name="mlp",
    )(x, ada, w1_b, w2_b, ln_g, ln_b)


def kernel(x, c, positions, lb_logits, w_in, w_out, w_ada, b_ada, ln_g, ln_b, hg_norm_w, m_conv_w, m_conv_b,
           m_dt_bias, m_a_log, m_d, m_norm_w, w1, w2):
    bsz, seq, _ = x.shape
    assert seq % CHUNK == 0

    pad_heads = ((0, 0), (0, LANES - M_HEADS))
    dtb3 = jnp.pad(m_dt_bias, pad_heads).reshape(DEPTH, 1, LANES)
    alog3 = jnp.pad(m_a_log, pad_heads).reshape(DEPTH, 1, LANES)
    dskip3 = jnp.repeat(m_d, M_HEAD_DIM, axis=1).reshape(DEPTH, 1, M_DIM)
    conv_b3 = m_conv_b.reshape(DEPTH, 1, M_CONV_DIM)
    norm_w3 = m_norm_w.reshape(DEPTH, 1, M_DIM)
    pos3 = positions.reshape(bsz, seq, 1)

    inv_freq = 1.0 / (ROPE_BASE ** jnp.linspace(0.0, 1.0, HEAD_DIM // 2, dtype=F32))
    invf2 = jnp.concatenate([inv_freq, inv_freq]).reshape(1, HEAD_DIM)
    idx = jnp.arange(CHUNK)
    tri = (idx[None, :] <= idx[:, None]).astype(BF16)
    expand = (jnp.arange(M_DIM)[None, :] // M_HEAD_DIM == jnp.arange(LANES)[:, None]).astype(BF16)

    ada = _ada(c, w_ada, b_ada).reshape(DEPTH * bsz, N_ADA, D_MODEL)
    cos2, sin2 = _rope(pos3, invf2)
    w_in_b = jnp.pad(w_in[0], ((0, 0), (0, IN_COLS_PAD - IN_COLS))).astype(BF16)
    for l in range(DEPTH):
        proj = _inproj(x, ada, w_in_b, l)
        o_ret, w_out_b, *rest = _retention(proj, cos2, sin2, w_out, w_in, l)
        if rest:
            w_in_b = rest[0]
        o_hg, w1_b = _hgrn2(proj, lb_logits, hg_norm_w, tri, w1, l)
        o_m, w2_b = _mamba2(proj, m_conv_w, conv_b3, dtb3, alog3, dskip3, norm_w3, tri, expand, w2, l)
        x = _outproj(x, o_ret, o_hg, o_m, w_out_b, ada, ln_g, ln_b, l)
        x = _mlp(x, ada, w1_b, w2_b, ln_g, ln_b, l)
    return x
```

```python
import functools
import math

import jax
import jax.numpy as jnp
from jax import lax
from jax.experimental import pallas as pl
from jax.experimental.pallas import tpu as pltpu

F32 = jnp.float32
BF16 = jnp.bfloat16

D_MODEL = 2048
DEPTH = 4
RET_HEADS = 4
RET_DIM = 512
HEAD_DIM = 128
ROPE_BASE = 10000.0
HG_HEADS = 4
HG_DIM = 512
M_DIM = 1024
M_HEAD_DIM = 64
M_HEADS = 16
M_GROUPS = 2
M_STATE = 128
M_CONV = 4
M_CONV_DIM = 1536
M_BC = M_CONV_DIM - M_DIM
D_FF = 4 * D_MODEL
IN_COLS = 6672
DEEPNORM_ALPHA = (2 * DEPTH) ** 0.25
LN_EPS = 1e-5
RMS_EPS = 1e-6
N_ADA = 6

LANES = 128
SUBLANES = 8
IN_COLS_PAD = 6912
VMEM_LIMIT = 56 * 2 ** 20

COL_RET = (0, 4 * RET_DIM)
COL_HG = (4 * RET_DIM, 4 * HG_DIM)
COL_ZX = (4 * RET_DIM + 4 * HG_DIM, 2 * M_DIM)
COL_BC = (COL_ZX[0] + 2 * M_DIM, M_BC)
COL_DT = (COL_BC[0] + M_BC, LANES)

ADA_TN = 1024
ROPE_TM = 512
INPROJ_TM, INPROJ_TN = 1024, 768
OUTPROJ_TM = 512
MLP_TM, MLP_TF = 1024, 512
MLP_TN = 512
LN_ROWS = 128

CHUNK = 128
HG_LEVELS = tuple(1 << i for i in range(int(math.log2(CHUNK))))


def _params(*sem):
    return pltpu.CompilerParams(dimension_semantics=sem, vmem_limit_bytes=VMEM_LIMIT)


def _silu(x):
    return x * jax.nn.sigmoid(x)


def _dot(a, b):
    return jnp.dot(a, b, preferred_element_type=F32)


def _dot_nt(a, b):
    return lax.dot_general(a, b, (((1,), (1,)), ((), ())), preferred_element_type=F32)


def _dot_tn(a, b):
    return lax.dot_general(a, b, (((0,), (0,)), ((), ())), preferred_element_type=F32)


def _split3(x):
    hi = x.astype(BF16)
    r = x - hi.astype(F32)
    mid = r.astype(BF16)
    lo = (r - mid.astype(F32)).astype(BF16)
    return hi, mid, lo


def _dot_exact_rhs(m_bf16, x):
    hi, mid, lo = _split3(x)
    return _dot(m_bf16, hi) + _dot(m_bf16, mid) + _dot(m_bf16, lo)


def _dot_exact_lhs(x, m_bf16):
    hi, mid, lo = _split3(x)
    return _dot(hi, m_bf16) + _dot(mid, m_bf16) + _dot(lo, m_bf16)


def _layer_norm(v, g, b):
    mu = jnp.mean(v, axis=-1, keepdims=True)
    d = v - mu
    var = jnp.mean(d * d, axis=-1, keepdims=True)
    return d * lax.rsqrt(var + LN_EPS) * g + b


def _win_slab_rows(n_steps):
    return LANES * pl.cdiv(IN_COLS_PAD, LANES * n_steps)


def _ada_kernel(c_ref, w_ref, b_ref, o_ref):
    cond = _silu(c_ref[...]).astype(BF16)
    o_ref[0] = _dot(cond, w_ref[0].astype(BF16)) + b_ref[0]


def _ada(c, w_ada, b_ada):
    bsz = c.shape[0]
    tn = ADA_TN
    return pl.pallas_call(
        _ada_kernel,
        grid=(DEPTH, N_ADA * D_MODEL // tn),
        in_specs=[
            pl.BlockSpec((bsz, D_MODEL), lambda l, n: (0, 0)),
            pl.BlockSpec((1, D_MODEL, tn), lambda l, n: (l, 0, n)),
            pl.BlockSpec((1, 1, tn), lambda l, n: (l, 0, n)),
        ],
        out_specs=pl.BlockSpec((1, bsz, tn), lambda l, n: (l, 0, n)),
        out_shape=jax.ShapeDtypeStruct((DEPTH, bsz, N_ADA * D_MODEL), F32),
        compiler_params=_params("arbitrary", "arbitrary"),
        name="ada",
    )(c, w_ada, b_ada.reshape(DEPTH, 1, N_ADA * D_MODEL))


def _rope_kernel(pos_ref, invf_ref, cos_ref, sin_ref):
    ang = pos_ref[0].astype(F32) * invf_ref[...]
    lane = lax.broadcasted_iota(jnp.int32, (1, HEAD_DIM), 1)
    cos_ref[0] = jnp.cos(ang)
    sin_ref[0] = jnp.where(lane < HEAD_DIM // 2, -1.0, 1.0) * jnp.sin(ang)


def _rope(pos3, invf2):
    bsz, seq, _ = pos3.shape
    tm = min(ROPE_TM, seq)
    tab = pl.BlockSpec((1, tm, HEAD_DIM), lambda b, m: (b, m, 0))
    shape = jax.ShapeDtypeStruct((bsz, seq, HEAD_DIM), F32)
    return pl.pallas_call(
        _rope_kernel,
        grid=(bsz, seq // tm),
        in_specs=[pl.BlockSpec((1, tm, 1), lambda b, m: (b, m, 0)),
                  pl.BlockSpec((1, HEAD_DIM), lambda b, m: (0, 0))],
        out_specs=(tab, tab),
        out_shape=(shape, shape),
        compiler_params=_params("arbitrary", "arbitrary"),
        name="rope",
    )(pos3, invf2)


def _inproj_kernel(x_ref, ada_ref, wt_ref, o_ref, h_ref):
    @pl.when(pl.program_id(2) == 0)
    def _():
        sh = ada_ref[0, 0:1, :]
        sc = ada_ref[0, 1:2, :]
        h_ref[...] = (x_ref[0] * (1.0 + sc) + sh).astype(BF16)

    o_ref[0] = _dot_nt(h_ref[...], wt_ref[...])


def _inproj(x, ada, w_in_t, layer):
    bsz, seq, _ = x.shape
    tm = min(INPROJ_TM, seq)
    tn = INPROJ_TN
    return pl.pallas_call(
        _inproj_kernel,
        grid=(bsz, seq // tm, IN_COLS_PAD // tn),
        in_specs=[
            pl.BlockSpec((1, tm, D_MODEL), lambda b, m, n: (b, m, 0)),
            pl.BlockSpec((1, N_ADA, D_MODEL), lambda b, m, n: (layer * bsz + b, 0, 0)),
            pl.BlockSpec((tn, D_MODEL), lambda b, m, n: (n, 0)),
        ],
        out_specs=pl.BlockSpec((1, tm, tn), lambda b, m, n: (b, m, n)),
        out_shape=jax.ShapeDtypeStruct((bsz, seq, IN_COLS_PAD), F32),
        scratch_shapes=[pltpu.VMEM((tm, D_MODEL), BF16)],
        compiler_params=_params("arbitrary", "arbitrary", "arbitrary"),
        name="inproj",
    )(x, ada, w_in_t)


def _retention_chunk(cos_ref, sin_ref, p_ref, o_ref, s_ref):
    c = CHUNK
    cos2 = cos_ref[0]
    sin2 = sin_ref[0]
    row = lax.broadcasted_iota(jnp.int32, (c, 1), 0)
    col = lax.broadcasted_iota(jnp.int32, (1, c), 1)
    diff = row - col
    rowf = row.astype(F32)

    def rot(t):
        return t * cos2 + pltpu.roll(t, HEAD_DIM // 2, 1) * sin2

    for h in range(RET_HEADS):
        log_gamma = math.log(1.0 - 2.0 ** (-5.0 - h))
        q = rot(p_ref[0, :, h * HEAD_DIM:(h + 1) * HEAD_DIM])
        k = rot(p_ref[0, :, RET_DIM + h * HEAD_DIM:RET_DIM + (h + 1) * HEAD_DIM]) * (HEAD_DIM ** -0.5)
        v = p_ref[0, :, 2 * RET_DIM + h * HEAD_DIM:2 * RET_DIM + (h + 1) * HEAD_DIM].astype(BF16)
        g = p_ref[0, :, 3 * RET_DIM + h * HEAD_DIM:3 * RET_DIM + (h + 1) * HEAD_DIM]
        decay = jnp.where(diff >= 0, jnp.exp(diff.astype(F32) * log_gamma), 0.0)
        scores = _dot_nt(q.astype(BF16), k.astype(BF16)) * decay
        st = s_ref[h]
        y = _dot(scores.astype(BF16), v)
        y = y + _dot((q * jnp.exp((rowf + 1.0) * log_gamma)).astype(BF16), st.astype(BF16))
        k_end = k * jnp.exp((c - 1.0 - rowf) * log_gamma)
        s_ref[h] = math.exp(c * log_gamma) * st + _dot_tn(k_end.astype(BF16), v)
        o = y * lax.rsqrt(jnp.mean(y * y, axis=-1, keepdims=True) + RMS_EPS)
        o_ref[0, :, h * HEAD_DIM:(h + 1) * HEAD_DIM] = (_silu(g) * o).astype(BF16)


def _hgrn2_chunk(layer, p_ref, lbl_ref, nw_ref, tri_ref, o_ref, s_ref):
    c = CHUNK
    lbl = lbl_ref[...]
    e = jnp.exp(lbl - jnp.max(lbl, axis=0, keepdims=True))
    p = e / jnp.sum(e, axis=0, keepdims=True)
    lb = jnp.zeros((1, HG_DIM), F32)
    for d in range(1, layer + 1):
        lb = lb + p[d:d + 1, :]

    q = p_ref[0, :, 0:HG_DIM]
    f = lb + (1.0 - lb) * jax.nn.sigmoid(p_ref[0, :, HG_DIM:2 * HG_DIM])
    log_f = jnp.log(f)
    k = 1.0 - f
    v_bf = p_ref[0, :, 2 * HG_DIM:3 * HG_DIM].astype(BF16)
    q_bf = q.astype(BF16)
    k_bf = k.astype(BF16)

    cum = _dot_exact_rhs(tri_ref[...], log_f)
    row = lax.broadcasted_iota(jnp.int32, (c, 1), 0)
    col = lax.broadcasted_iota(jnp.int32, (1, c), 1)
    r_sub = row & (SUBLANES - 1)

    def pick_row(x, block, j):
        x3 = x.reshape(c // block, block, HG_DIM)[:, j:j + 1, :]
        return jnp.broadcast_to(x3, (c // block, block, HG_DIM)).reshape(c, HG_DIM)

    merged, masks = [], []
    for b in HG_LEVELS:
        sh = int(math.log2(b))
        upper = ((row >> sh) & 1) == 1
        if b == 1:
            m = jnp.where(upper, q * f, k)
        else:
            if 2 * b >= SUBLANES:
                ref = pick_row(cum, 2 * b, b - 1)
            else:
                ref = jnp.where(r_sub < 2 * b, pick_row(cum, SUBLANES, b - 1), pick_row(cum, SUBLANES, 3 * b - 1))
            m = jnp.where(upper, q, k) * jnp.exp(-jnp.abs(cum - ref))
        merged.append(m.astype(BF16))
        masks.append(((row >> (sh + 1)) == (col >> (sh + 1))) & upper & (((col >> sh) & 1) == 0))

    tail = cum[c - 1:c, :] - cum
    q_in = (q * jnp.exp(cum)).astype(BF16)
    k_st = (k * jnp.exp(tail)).astype(BF16)
    e_tot = jnp.exp(cum[c - 1:c, :])

    outs = []
    for h in range(HG_HEADS):
        sl = slice(h * HEAD_DIM, (h + 1) * HEAD_DIM)
        scores = jnp.where(row == col, _dot_nt(q_bf[:, sl], k_bf[:, sl]), 0.0)
        for lv in range(len(HG_LEVELS)):
            m_h = merged[lv][:, sl]
            scores = scores + jnp.where(masks[lv], _dot_nt(m_h, m_h), 0.0)
        st_t = s_ref[h]
        y = _dot(scores.astype(BF16), v_bf[:, sl]) + _dot_nt(q_in[:, sl], st_t.astype(BF16))
        s_ref[h] = st_t * e_tot[:, sl] + _dot_tn(v_bf[:, sl], k_st[:, sl])
        outs.append(y * lax.rsqrt(jnp.mean(y * y, axis=-1, keepdims=True) + RMS_EPS))
    o = jnp.concatenate(outs, axis=1) * nw_ref[0] * _silu(p_ref[0, :, 3 * HG_DIM:4 * HG_DIM])
    o_ref[0, :, RET_DIM:RET_DIM + HG_DIM] = o.astype(BF16)


def _mamba2_chunk(zx_ref, bc_ref, dt_ref, cw_ref, cb_ref, dtb_ref, alog_ref, dskip_ref, nw_ref, tri_ref, expand_ref,
                  o_ref, s_ref, ext_x_ref, ext_b_ref):
    c = CHUNK
    pad = SUBLANES

    def conv_silu(ext_ref, u, lo, hi):
        ext_ref[pad:pad + c, :] = u
        acc = cb_ref[0, :, lo:hi]
        for j in range(M_CONV):
            acc = acc + cw_ref[0, j:j + 1, lo:hi] * ext_ref[pad - (M_CONV - 1) + j:pad - (M_CONV - 1) + j + c, :]
        ext_ref[0:pad, :] = ext_ref[c:c + pad, :]
        return _silu(acc)

    xs = conv_silu(ext_x_ref, zx_ref[0, :, M_DIM:], 0, M_DIM)
    bc = conv_silu(ext_b_ref, bc_ref[0], M_DIM, M_CONV_DIM)

    dt = jax.nn.softplus(dt_ref[0] + dtb_ref[0])
    a = -jnp.exp(alog_ref[0])
    cum = _dot_exact_rhs(tri_ref[...], dt * a)
    cum_t = cum.T
    e_cum = jnp.exp(cum)
    w_state = dt * jnp.exp(cum[c - 1:c, :] - cum)
    spread = _dot_exact_lhs(jnp.concatenate([dt, e_cum, w_state], axis=0), expand_ref[...])
    dt_e, e_cum_e, w_state_e = spread[0:c], spread[c:2 * c], spread[2 * c:3 * c]
    x_dt = (xs * dt_e).astype(BF16)
    x_st = (xs * w_state_e).astype(BF16)

    row = lax.broadcasted_iota(jnp.int32, (c, 1), 0)
    col = lax.broadcasted_iota(jnp.int32, (1, c), 1)
    causal = row >= col
    lane = lax.broadcasted_iota(jnp.int32, (1, LANES), 1)
    gw = M_DIM // M_GROUPS
    heads_per_group = M_HEADS // M_GROUPS

    ys = []
    for g in range(M_GROUPS):
        b_g = bc[:, g * M_STATE:(g + 1) * M_STATE].astype(BF16)
        c_g = bc[:, (M_GROUPS + g) * M_STATE:(M_GROUPS + g + 1) * M_STATE].astype(BF16)
        gram = _dot_nt(c_g, b_g)
        st = s_ref[g]
        gsl = slice(g * gw, (g + 1) * gw)
        y_g = _dot(c_g, st.astype(BF16)) * e_cum_e[:, gsl]
        s_ref[g] = st * e_cum_e[c - 1:c, gsl] + _dot_tn(b_g, x_st[:, gsl])
        pieces = []
        for pr in range(heads_per_group // 2):
            psl = slice(g * gw + pr * LANES, g * gw + (pr + 1) * LANES)
            x_pair = x_dt[:, psl]
            acc = jnp.zeros((c, LANES), F32)
            for half in range(2):
                h = g * heads_per_group + 2 * pr + half
                seg = cum[:, h:h + 1] - cum_t[h:h + 1, :]
                decay = jnp.exp(jnp.where(causal, seg, -jnp.inf))
                in_head = (lane >= half * M_HEAD_DIM) & (lane < (half + 1) * M_HEAD_DIM)
                acc = acc + _dot((gram * decay).astype(BF16), jnp.where(in_head, x_pair, jnp.zeros_like(x_pair)))
            pieces.append(acc)
        ys.append(y_g + jnp.concatenate(pieces, axis=1))
    y = jnp.concatenate(ys, axis=1) + xs * dskip_ref[0]
    y = y * _silu(zx_ref[0, :, 0:M_DIM])
    outs = []
    for g in range(M_GROUPS):
        yg = y[:, g * gw:(g + 1) * gw]
        outs.append(yg * lax.rsqrt(jnp.mean(yg * yg, axis=-1, keepdims=True) + RMS_EPS))
    o_ref[0, :, RET_DIM + HG_DIM:] = (jnp.concatenate(outs, axis=1) * nw_ref[0]).astype(BF16)


def _mixers_kernel(layer, has_next, n_chunks, *refs):
    (cos_ref, sin_ref, ret_ref, hg_ref, zx_ref, bc_ref, dt_ref, lbl_ref, hnw_ref, cw_ref, cb_ref, dtb_ref, alog_ref,
     dskip_ref, mnw_ref, tri_ref, expand_ref, wout_ref, w1_ref, w2_ref) = refs[:20]
    refs = refs[20:]
    if has_next:
        win_ref, refs = refs[0], refs[1:]
    o_ref, wout_o_ref, w1_o_ref, w2_o_ref = refs[:4]
    refs = refs[4:]
    if has_next:
        win_o_ref, refs = refs[0], refs[1:]
    ret_s_ref, hg_s_ref, m_s_ref, ext_x_ref, ext_b_ref = refs

    wout_o_ref[...] = wout_ref[...].astype(BF16)
    w1_o_ref[...] = w1_ref[...].astype(BF16)
    w2_o_ref[...] = w2_ref[...].astype(BF16)
    if has_next:
        rows = win_o_ref.shape[0]
        step = pl.program_id(0) * n_chunks + pl.program_id(1)
        blk = jnp.minimum(step, pl.cdiv(IN_COLS_PAD, rows) - 1)
        n_idx = blk * rows + lax.broadcasted_iota(jnp.int32, (rows, 1), 0)
        win_o_ref[...] = jnp.where(n_idx < IN_COLS, win_ref[...], 0.0).astype(BF16)

    @pl.when(pl.program_id(1) == 0)
    def _():
        ret_s_ref[...] = jnp.zeros_like(ret_s_ref)
        hg_s_ref[...] = jnp.zeros_like(hg_s_ref)
        m_s_ref[...] = jnp.zeros_like(m_s_ref)
        ext_x_ref[0:SUBLANES, :] = jnp.zeros((SUBLANES, M_DIM), F32)
        ext_b_ref[0:SUBLANES, :] = jnp.zeros((SUBLANES, M_BC), F32)

    _retention_chunk(cos_ref, sin_ref, ret_ref, o_ref, ret_s_ref)
    _hgrn2_chunk(layer, hg_ref, lbl_ref, hnw_ref, tri_ref, o_ref, hg_s_ref)
    _mamba2_chunk(zx_ref, bc_ref, dt_ref, cw_ref, cb_ref, dtb_ref, alog_ref, dskip_ref, mnw_ref, tri_ref, expand_ref,
                  o_ref, m_s_ref, ext_x_ref, ext_b_ref)


def _mixers(proj, cos2, sin2, lb_logits, hg_norm_w3, conv_w, conv_b3, dtb3, alog3, dskip3, m_norm_w3, tri, expand,
            w_out, w1, w2, w_in_t, layer):
    bsz, seq, _ = proj.shape
    c = CHUNK
    n_chunks = seq // c
    n_steps = bsz * n_chunks
    has_next = layer + 1 < DEPTH

    def cols(first, width):
        assert first % width == 0
        return pl.BlockSpec((1, c, width), lambda b, i: (b, i, first // width))

    def lay(shape):
        return pl.BlockSpec((1,) + shape, lambda b, i: (layer, 0, 0))

    def const(shape):
        return pl.BlockSpec(shape, lambda b, i: (0,) * len(shape))

    def slab_in(w):
        rows = w.shape[1] // n_steps
        assert rows * n_steps == w.shape[1] and rows % (2 * SUBLANES) == 0
        return pl.BlockSpec((None, rows, w.shape[2]), lambda b, i: (layer, b * n_chunks + i, 0))

    def slab_out(w):
        rows = w.shape[1] // n_steps
        return (pl.BlockSpec((rows, w.shape[2]), lambda b, i: (b * n_chunks + i, 0)),
                jax.ShapeDtypeStruct(w.shape[1:], BF16))

    tab = pl.BlockSpec((1, c, HEAD_DIM), lambda b, i: (b, i, 0))
    in_specs = [
        tab, tab, cols(*COL_RET), cols(*COL_HG), cols(*COL_ZX), cols(*COL_BC), cols(*COL_DT),
        const((DEPTH, HG_DIM)), lay((1, HG_DIM)),
        lay((M_CONV, M_CONV_DIM)), lay((1, M_CONV_DIM)), lay((1, LANES)), lay((1, LANES)), lay((1, M_DIM)),
        lay((1, M_DIM)), const((c, c)), const((LANES, M_DIM)),
        slab_in(w_out), slab_in(w1), slab_in(w2),
    ]
    args = [cos2, sin2, proj, proj, proj, proj, proj, lb_logits, hg_norm_w3, conv_w, conv_b3, dtb3, alog3, dskip3,
            m_norm_w3, tri, expand, w_out, w1, w2]
    out_specs = [pl.BlockSpec((1, c, D_MODEL), lambda b, i: (b, i, 0))]
    out_shape = [jax.ShapeDtypeStruct((bsz, seq, D_MODEL), BF16)]
    for w in (w_out, w1, w2):
        spec, shape = slab_out(w)
        out_specs.append(spec)
        out_shape.append(shape)
    if has_next:
        rows = _win_slab_rows(n_steps)
        last_in = pl.cdiv(IN_COLS, rows) - 1
        last_out = pl.cdiv(IN_COLS_PAD, rows) - 1
        in_specs.append(pl.BlockSpec((None, rows, D_MODEL),
                                     lambda b, i: (layer + 1, jnp.minimum(b * n_chunks + i, last_in), 0)))
        args.append(w_in_t)
        out_specs.append(pl.BlockSpec((rows, D_MODEL), lambda b, i: (jnp.minimum(b * n_chunks + i, last_out), 0)))
        out_shape.append(jax.ShapeDtypeStruct(((last_out + 1) * rows, D_MODEL), BF16))
    return pl.pallas_call(
        functools.partial(_mixers_kernel, layer, has_next, n_chunks),
        grid=(bsz, n_chunks),
        in_specs=in_specs,
        out_specs=out_specs,
        out_shape=out_shape,
        scratch_shapes=[
            pltpu.VMEM((RET_HEADS, HEAD_DIM, HEAD_DIM), F32),
            pltpu.VMEM((HG_HEADS, HEAD_DIM, HEAD_DIM), F32),
            pltpu.VMEM((M_GROUPS, M_STATE, M_DIM // M_GROUPS), F32),
            pltpu.VMEM((SUBLANES + c, M_DIM), F32),
            pltpu.VMEM((SUBLANES + c, M_BC), F32),
        ],
        compiler_params=_params("arbitrary", "arbitrary"),
        name="mixers",
    )(*args)


def _residual_norm(x_ref, o_ref, gate, g, b):
    for r in range(0, o_ref.shape[1], LN_ROWS):
        rs = slice(r, r + LN_ROWS)
        o_ref[0, rs, :] = _layer_norm(DEEPNORM_ALPHA * x_ref[0, rs, :] + (1.0 + gate) * o_ref[0, rs, :], g, b)


def _outproj_kernel(x_ref, mix_ref, w_ref, ada_ref, g_ref, b_ref, o_ref):
    o_ref[0] = _dot(mix_ref[0], w_ref[...])
    _residual_norm(x_ref, o_ref, ada_ref[0, 2:3, :], g_ref[0, 0:1, :], b_ref[0, 0:1, :])


def _outproj(x, mix, w_out_b, ada, ln_g, ln_b, layer):
    bsz, seq, _ = x.shape
    tm = min(OUTPROJ_TM, seq)
    rows = pl.BlockSpec((1, tm, D_MODEL), lambda b, m: (b, m, 0))
    return pl.pallas_call(
        _outproj_kernel,
        grid=(bsz, seq // tm),
        in_specs=[
            rows, rows,
            pl.BlockSpec((D_MODEL, D_MODEL), lambda b, m: (0, 0)),
            pl.BlockSpec((1, N_ADA, D_MODEL), lambda b, m: (layer * bsz + b, 0, 0)),
            pl.BlockSpec((1, 2, D_MODEL), lambda b, m: (layer, 0, 0)),
            pl.BlockSpec((1, 2, D_MODEL), lambda b, m: (layer, 0, 0)),
        ],
        out_specs=rows,
        out_shape=jax.ShapeDtypeStruct(x.shape, F32),
        compiler_params=_params("arbitrary", "arbitrary"),
        name="outproj",
    )(x, mix, w_out_b, ada, ln_g, ln_b)


def _mlp_kernel(x_ref, ada_ref, w1_ref, w2_ref, g_ref, b_ref, o_ref, h_ref):
    f = pl.program_id(2)

    @pl.when(f == 0)
    def _():
        sh = ada_ref[0, 3:4, :]
        sc = ada_ref[0, 4:5, :]
        h_ref[...] = (x_ref[0] * (1.0 + sc) + sh).astype(BF16)

    u = jnp.maximum(_dot(h_ref[...], w1_ref[...]), 0.0)
    u2 = (u * u).astype(BF16)

    @pl.when(f == 0)
    def _():
        for n in range(0, D_MODEL, MLP_TN):
            o_ref[0, :, n:n + MLP_TN] = _dot(u2, w2_ref[:, n:n + MLP_TN])

    @pl.when(f > 0)
    def _():
        for n in range(0, D_MODEL, MLP_TN):
            o_ref[0, :, n:n + MLP_TN] += _dot(u2, w2_ref[:, n:n + MLP_TN])

    @pl.when(f == pl.num_programs(2) - 1)
    def _():
        _residual_norm(x_ref, o_ref, ada_ref[0, 5:6, :], g_ref[0, 1:2, :], b_ref[0, 1:2, :])


def _mlp(x, ada, w1_b, w2_b, ln_g, ln_b, layer):
    bsz, seq, _ = x.shape
    tm = min(MLP_TM, seq)
    tf = MLP_TF
    return pl.pallas_call(
        _mlp_kernel,
        grid=(bsz, seq // tm, D_FF // tf),
        in_specs=[
            pl.BlockSpec((1, tm, D_MODEL), lambda b, m, f: (b, m, 0)),
            pl.BlockSpec((1, N_ADA, D_MODEL), lambda b, m, f: (layer * bsz + b, 0, 0)),
            pl.BlockSpec((D_MODEL, tf), lambda b, m, f: (0, f)),
            pl.BlockSpec((tf, D_MODEL), lambda b, m, f: (f, 0)),
            pl.BlockSpec((1, 2, D_MODEL), lambda b, m, f: (layer, 0, 0)),
            pl.BlockSpec((1, 2, D_MODEL), lambda b, m, f: (layer, 0, 0)),
        ],
        out_specs=pl.BlockSpec((1, tm, D_MODEL), lambda b, m, f: (b, m, 0)),
        out_shape=jax.ShapeDtypeStruct(x.shape, F32),
        scratch_shapes=[pltpu.VMEM((tm, D_MODEL), BF16)],
        compiler_params=_params("arbitrary", "arbitrary", "arbitrary"),
        name="mlp",
    )(x, ada, w1_b, w2_b, ln_g, ln_b)


def kernel(x, c, positions, lb_logits, w_in, w_out, w_ada, b_ada, ln_g, ln_b, hg_norm_w, m_conv_w, m_conv_b,
           m_dt_bias, m_a_log, m_d, m_norm_w, w1, w2):
    bsz, seq, _ = x.shape
    assert seq % CHUNK == 0
    n_steps = bsz * (seq // CHUNK)

    pad_heads = ((0, 0), (0, LANES - M_HEADS))
    dtb3 = jnp.pad(m_dt_bias, pad_heads).reshape(DEPTH, 1, LANES)
    alog3 = jnp.pad(m_a_log, pad_heads).reshape(DEPTH, 1, LANES)
    dskip3 = jnp.repeat(m_d, M_HEAD_DIM, axis=1).reshape(DEPTH, 1, M_DIM)
    conv_b3 = m_conv_b.reshape(DEPTH, 1, M_CONV_DIM)
    m_norm_w3 = m_norm_w.reshape(DEPTH, 1, M_DIM)
    hg_norm_w3 = hg_norm_w.reshape(DEPTH, 1, HG_DIM)
    pos3 = positions.reshape(bsz, seq, 1)

    inv_freq = 1.0 / (ROPE_BASE ** jnp.linspace(0.0, 1.0, HEAD_DIM // 2, dtype=F32))
    invf2 = jnp.concatenate([inv_freq, inv_freq]).reshape(1, HEAD_DIM)
    idx = jnp.arange(CHUNK)
    tri = (idx[None, :] <= idx[:, None]).astype(BF16)
    expand = (jnp.arange(M_DIM)[None, :] // M_HEAD_DIM == jnp.arange(LANES)[:, None]).astype(BF16)

    ada = _ada(c, w_ada, b_ada).reshape(DEPTH * bsz, N_ADA, D_MODEL)
    cos2, sin2 = _rope(pos3, invf2)
    w_in_t = jnp.swapaxes(w_in, 1, 2)
    rows = _win_slab_rows(n_steps)
    w_in_b = jnp.pad(w_in_t[0], ((0, pl.cdiv(IN_COLS_PAD, rows) * rows - IN_COLS), (0, 0))).astype(BF16)
    for l in range(DEPTH):
        proj = _inproj(x, ada, w_in_b, l)
        mix, w_out_b, w1_b, w2_b, *rest = _mixers(proj, cos2, sin2, lb_logits, hg_norm_w3, m_conv_w, conv_b3, dtb3,
                                                  alog3, dskip3, m_norm_w3, tri, expand, w_out, w1, w2, w_in_t, l)
        if rest:
            w_in_b = rest[0]
        x = _outproj(x, mix, w_out_b, ada, ln_g, ln_b, l)
        x = _mlp(x, ada, w1_b, w2_b, ln_g, ln_b, l)
    return x
```

```python
import functools
import math

import jax
import jax.numpy as jnp
from jax import lax
from jax.experimental import pallas as pl
from jax.experimental.pallas import tpu as pltpu

F32 = jnp.float32
BF16 = jnp.bfloat16

D_MODEL = 2048
DEPTH = 4
RET_HEADS = 4
RET_DIM = 512
HEAD_DIM = 128
ROPE_BASE = 10000.0
HG_HEADS = 4
HG_DIM = 512
M_DIM = 1024
M_HEAD_DIM = 64
M_HEADS = 16
M_GROUPS = 2
M_STATE = 128
M_CONV = 4
M_CONV_DIM = 1536
M_BC = M_CONV_DIM - M_DIM
D_FF = 4 * D_MODEL
IN_COLS = 6672
DEEPNORM_ALPHA = (2 * DEPTH) ** 0.25
LN_EPS = 1e-5
RMS_EPS = 1e-6
N_ADA = 6
LOG2E = math.log2(math.e)

LANES = 128
SUBLANES = 8
IN_COLS_PAD = 6912
VMEM_LIMIT = 56 * 2 ** 20

COL_RET = (0, 4 * RET_DIM)
COL_HG = (4 * RET_DIM, 4 * HG_DIM)
COL_ZX = (4 * RET_DIM + 4 * HG_DIM, 2 * M_DIM)
COL_BC = (COL_ZX[0] + 2 * M_DIM, M_BC)
COL_DT = (COL_BC[0] + M_BC, LANES)

ADA_TN = 1024
ROPE_TM = 512
INPROJ_TM, INPROJ_TN = 1024, 768
OUTPROJ_TM = 512
OUTPROJ_TR = 256
MLP_TM, MLP_TF = 1024, 512
MLP_TN = 512
MLP_TR = 512
LN_ROWS = 128

CHUNK = 128
HG_LEVELS = tuple(1 << i for i in range(int(math.log2(CHUNK))))


def _params(*sem):
    return pltpu.CompilerParams(dimension_semantics=sem, vmem_limit_bytes=VMEM_LIMIT)


def _silu(x):
    return x * jax.nn.sigmoid(x)


def _dot(a, b):
    return jnp.dot(a, b, preferred_element_type=F32)


def _dot_nt(a, b):
    return lax.dot_general(a, b, (((1,), (1,)), ((), ())), preferred_element_type=F32)


def _dot_tn(a, b):
    return lax.dot_general(a, b, (((0,), (0,)), ((), ())), preferred_element_type=F32)


def _split(x, pieces):
    out = []
    for _ in range(pieces - 1):
        p = x.astype(BF16)
        out.append(p)
        x = x - p.astype(F32)
    out.append(x.astype(BF16))
    return out


def _dot_split_rhs(m_bf16, x, pieces):
    return functools.reduce(jnp.add, [_dot(m_bf16, p) for p in _split(x, pieces)])


def _dot_split_lhs(x, m_bf16, pieces):
    return functools.reduce(jnp.add, [_dot(p, m_bf16) for p in _split(x, pieces)])


def _layer_norm(v, g, b):
    mu = jnp.mean(v, axis=-1, keepdims=True)
    d = v - mu
    var = jnp.mean(d * d, axis=-1, keepdims=True)
    return d * lax.rsqrt(var + LN_EPS) * g + b


def _win_slab_rows(n_steps):
    return LANES * pl.cdiv(IN_COLS_PAD, LANES * n_steps)


def _win_specs(layer, n_steps, step_of):
    rows = _win_slab_rows(n_steps)
    last_in = pl.cdiv(IN_COLS, rows) - 1
    last_out = pl.cdiv(IN_COLS_PAD, rows) - 1
    assert last_out < n_steps
    return (pl.BlockSpec((None, rows, D_MODEL), lambda *g: (layer, jnp.minimum(step_of(*g), last_in), 0)),
            pl.BlockSpec((rows, D_MODEL), lambda *g: (jnp.minimum(step_of(*g), last_out), 0)),
            jax.ShapeDtypeStruct(((last_out + 1) * rows, D_MODEL), BF16))


def _win_convert(win_ref, win_o_ref, step):
    rows = win_o_ref.shape[0]
    blk = jnp.minimum(step, pl.cdiv(IN_COLS_PAD, rows) - 1)
    n_idx = blk * rows + lax.broadcasted_iota(jnp.int32, (rows, 1), 0)
    win_o_ref[...] = jnp.where(n_idx < IN_COLS, win_ref[...], 0.0).astype(BF16)


def _ada_kernel(c_ref, w_ref, b_ref, o_ref):
    cond = _silu(c_ref[...]).astype(BF16)
    o_ref[0] = _dot(cond, w_ref[0].astype(BF16)) + b_ref[0]


def _ada(c, w_ada, b_ada):
    bsz = c.shape[0]
    tn = ADA_TN
    return pl.pallas_call(
        _ada_kernel,
        grid=(DEPTH, N_ADA * D_MODEL // tn),
        in_specs=[
            pl.BlockSpec((bsz, D_MODEL), lambda l, n: (0, 0)),
            pl.BlockSpec((1, D_MODEL, tn), lambda l, n: (l, 0, n)),
            pl.BlockSpec((1, 1, tn), lambda l, n: (l, 0, n)),
        ],
        out_specs=pl.BlockSpec((1, bsz, tn), lambda l, n: (l, 0, n)),
        out_shape=jax.ShapeDtypeStruct((DEPTH, bsz, N_ADA * D_MODEL), F32),
        compiler_params=_params("arbitrary", "arbitrary"),
        name="ada",
    )(c, w_ada, b_ada.reshape(DEPTH, 1, N_ADA * D_MODEL))


def _rope_kernel(n_tiles, pos_ref, invf_ref, win_ref, cos_ref, sin_ref, win_o_ref):
    _win_convert(win_ref, win_o_ref, pl.program_id(0) * n_tiles + pl.program_id(1))
    ang = pos_ref[0].astype(F32) * invf_ref[...]
    lane = lax.broadcasted_iota(jnp.int32, (1, HEAD_DIM), 1)
    cos_ref[0] = jnp.cos(ang)
    sin_ref[0] = jnp.where(lane < HEAD_DIM // 2, -1.0, 1.0) * jnp.sin(ang)


def _rope(pos3, invf2, w_in_t):
    bsz, seq, _ = pos3.shape
    tm = min(ROPE_TM, seq)
    n_tiles = seq // tm
    tab = pl.BlockSpec((1, tm, HEAD_DIM), lambda b, m: (b, m, 0))
    shape = jax.ShapeDtypeStruct((bsz, seq, HEAD_DIM), F32)
    win_in, win_out, win_shape = _win_specs(0, bsz * n_tiles, lambda b, m: b * n_tiles + m)
    return pl.pallas_call(
        functools.partial(_rope_kernel, n_tiles),
        grid=(bsz, n_tiles),
        in_specs=[pl.BlockSpec((1, tm, 1), lambda b, m: (b, m, 0)),
                  pl.BlockSpec((1, HEAD_DIM), lambda b, m: (0, 0)),
                  win_in],
        out_specs=(tab, tab, win_out),
        out_shape=(shape, shape, win_shape),
        compiler_params=_params("arbitrary", "arbitrary"),
        name="rope",
    )(pos3, invf2, w_in_t)


def _inproj_kernel(x_ref, ada_ref, wt_ref, o_ref, h_ref):
    @pl.when(pl.program_id(2) == 0)
    def _():
        sh = ada_ref[0, 0:1, :]
        sc = ada_ref[0, 1:2, :]
        h_ref[...] = (x_ref[0] * (1.0 + sc) + sh).astype(BF16)

    o_ref[0] = _dot_nt(h_ref[...], wt_ref[...])


def _inproj(x, ada, w_in_t, layer):
    bsz, seq, _ = x.shape
    tm = min(INPROJ_TM, seq)
    tn = INPROJ_TN
    return pl.pallas_call(
        _inproj_kernel,
        grid=(bsz, seq // tm, IN_COLS_PAD // tn),
        in_specs=[
            pl.BlockSpec((1, tm, D_MODEL), lambda b, m, n: (b, m, 0)),
            pl.BlockSpec((1, N_ADA, D_MODEL), lambda b, m, n: (layer * bsz + b, 0, 0)),
            pl.BlockSpec((tn, D_MODEL), lambda b, m, n: (n, 0)),
        ],
        out_specs=pl.BlockSpec((1, tm, tn), lambda b, m, n: (b, m, n)),
        out_shape=jax.ShapeDtypeStruct((bsz, seq, IN_COLS_PAD), F32),
        scratch_shapes=[pltpu.VMEM((tm, D_MODEL), BF16)],
        compiler_params=_params("arbitrary", "arbitrary", "arbitrary"),
        name="inproj",
    )(x, ada, w_in_t)


def _retention_tables():
    c = CHUNK
    i = jnp.arange(c, dtype=F32)[:, None]
    j = jnp.arange(c, dtype=F32)[None, :]
    ones = jnp.ones((1, HEAD_DIM), F32)
    tabs = []
    for h in range(RET_HEADS):
        log_gamma = math.log(1.0 - 2.0 ** (-5.0 - h))
        decay = jnp.where(i >= j, jnp.exp((i - j) * log_gamma), 0.0)
        tabs.append(jnp.stack([decay, jnp.exp((i + 1.0) * log_gamma) * ones, jnp.exp((c - 1.0 - i) * log_gamma) * ones]))
    return jnp.stack(tabs)


def _retention_chunk(cos_ref, sin_ref, tab_ref, p_ref, o_ref, s_ref):
    c = CHUNK
    cos2 = cos_ref[0]
    sin2 = sin_ref[0]

    def rot(t):
        return t * cos2 + pltpu.roll(t, HEAD_DIM // 2, 1) * sin2

    for h in range(RET_HEADS):
        log_gamma = math.log(1.0 - 2.0 ** (-5.0 - h))
        q = rot(p_ref[0, :, h * HEAD_DIM:(h + 1) * HEAD_DIM])
        k = rot(p_ref[0, :, RET_DIM + h * HEAD_DIM:RET_DIM + (h + 1) * HEAD_DIM]) * (HEAD_DIM ** -0.5)
        v = p_ref[0, :, 2 * RET_DIM + h * HEAD_DIM:2 * RET_DIM + (h + 1) * HEAD_DIM].astype(BF16)
        g = p_ref[0, :, 3 * RET_DIM + h * HEAD_DIM:3 * RET_DIM + (h + 1) * HEAD_DIM]
        scores = _dot_nt(q.astype(BF16), k.astype(BF16)) * tab_ref[h, 0]
        st = s_ref[h]
        y = _dot(scores.astype(BF16), v) + _dot((q * tab_ref[h, 1]).astype(BF16), st.astype(BF16))
        s_ref[h] = math.exp(c * log_gamma) * st + _dot_tn((k * tab_ref[h, 2]).astype(BF16), v)
        o = y * lax.rsqrt(jnp.mean(y * y, axis=-1, keepdims=True) + RMS_EPS)
        o_ref[0, :, h * HEAD_DIM:(h + 1) * HEAD_DIM] = (_silu(g) * o).astype(BF16)


def _hgrn2_chunk(layer, p_ref, lbl_ref, nw_ref, tri_ref, o_ref, s_ref):
    c = CHUNK
    lbl = lbl_ref[...]
    e = jnp.exp(lbl - jnp.max(lbl, axis=0, keepdims=True))
    p = e / jnp.sum(e, axis=0, keepdims=True)
    lb = jnp.zeros((1, HG_DIM), F32)
    for d in range(1, layer + 1):
        lb = lb + p[d:d + 1, :]

    q = p_ref[0, :, 0:HG_DIM]
    f = lb + (1.0 - lb) * jax.nn.sigmoid(p_ref[0, :, HG_DIM:2 * HG_DIM])
    log_f = jnp.log(f)
    k = 1.0 - f
    v_bf = p_ref[0, :, 2 * HG_DIM:3 * HG_DIM].astype(BF16)
    q_bf = q.astype(BF16)
    k_bf = k.astype(BF16)

    cum = _dot_split_rhs(tri_ref[...], log_f, 3)
    row = lax.broadcasted_iota(jnp.int32, (c, 1), 0)
    col = lax.broadcasted_iota(jnp.int32, (1, c), 1)
    r_sub = row & (SUBLANES - 1)

    def pick_row(x, block, j):
        x3 = x.reshape(c // block, block, HG_DIM)[:, j:j + 1, :]
        return jnp.broadcast_to(x3, (c // block, block, HG_DIM)).reshape(c, HG_DIM)

    merged, masks = [], []
    for b in HG_LEVELS:
        sh = int(math.log2(b))
        upper = ((row >> sh) & 1) == 1
        if b == 1:
            m = jnp.where(upper, q * f, k)
        else:
            if 2 * b >= SUBLANES:
                ref = pick_row(cum, 2 * b, b - 1)
            else:
                ref = jnp.where(r_sub < 2 * b, pick_row(cum, SUBLANES, b - 1), pick_row(cum, SUBLANES, 3 * b - 1))
            m = jnp.where(upper, q, k) * jnp.exp2((cum - ref) * jnp.where(upper, LOG2E, -LOG2E))
        merged.append(m.astype(BF16))
        masks.append(((row >> (sh + 1)) == (col >> (sh + 1))) & upper & (((col >> sh) & 1) == 0))

    tail = cum[c - 1:c, :] - cum
    q_in = (q * jnp.exp(cum)).astype(BF16)
    k_st = (k * jnp.exp(tail)).astype(BF16)
    e_tot = jnp.exp(cum[c - 1:c, :])

    outs = []
    for h in range(HG_HEADS):
        sl = slice(h * HEAD_DIM, (h + 1) * HEAD_DIM)
        scores = jnp.where(row == col, _dot_nt(q_bf[:, sl], k_bf[:, sl]), 0.0)
        for lv in range(len(HG_LEVELS)):
            m_h = merged[lv][:, sl]
            scores = jnp.where(masks[lv], _dot_nt(m_h, m_h), scores)
        st_t = s_ref[h]
        y = _dot(scores.astype(BF16), v_bf[:, sl]) + _dot_nt(q_in[:, sl], st_t.astype(BF16))
        s_ref[h] = st_t * e_tot[:, sl] + _dot_tn(v_bf[:, sl], k_st[:, sl])
        outs.append(y * lax.rsqrt(jnp.mean(y * y, axis=-1, keepdims=True) + RMS_EPS))
    o = jnp.concatenate(outs, axis=1) * nw_ref[0] * _silu(p_ref[0, :, 3 * HG_DIM:4 * HG_DIM])
    o_ref[0, :, RET_DIM:RET_DIM + HG_DIM] = o.astype(BF16)


def _mamba2_chunk(zx_ref, bc_ref, dt_ref, cw_ref, cb_ref, dtb_ref, alog_ref, dskip_ref, nw_ref, tri_ref, expand_ref,
                  o_ref, s_ref, ext_x_ref, ext_b_ref):
    c = CHUNK
    pad = SUBLANES

    def conv_silu(ext_ref, u, lo, hi):
        ext_ref[pad:pad + c, :] = u
        acc = cb_ref[0, :, lo:hi]
        for j in range(M_CONV):
            acc = acc + cw_ref[0, j:j + 1, lo:hi] * ext_ref[pad - (M_CONV - 1) + j:pad - (M_CONV - 1) + j + c, :]
        ext_ref[0:pad, :] = ext_ref[c:c + pad, :]
        return _silu(acc)

    xs = conv_silu(ext_x_ref, zx_ref[0, :, M_DIM:], 0, M_DIM)
    bc = conv_silu(ext_b_ref, bc_ref[0], M_DIM, M_CONV_DIM)

    dt = jax.nn.softplus(dt_ref[0] + dtb_ref[0])
    a = -jnp.exp(alog_ref[0])
    cum = _dot_split_rhs(tri_ref[...], dt * a, 3)
    cum_t = cum.T
    e_cum = jnp.exp(cum)
    w_state = dt * jnp.exp(cum[c - 1:c, :] - cum)
    spread = _dot_split_lhs(jnp.concatenate([dt, e_cum, w_state], axis=0), expand_ref[...], 2)
    dt_e, e_cum_e, w_state_e = spread[0:c], spread[c:2 * c], spread[2 * c:3 * c]
    x_dt = (xs * dt_e).astype(BF16)
    x_st = (xs * w_state_e).astype(BF16)

    row = lax.broadcasted_iota(jnp.int32, (c, 1), 0)
    col = lax.broadcasted_iota(jnp.int32, (1, c), 1)
    causal = row >= col
    lane = lax.broadcasted_iota(jnp.int32, (1, LANES), 1)
    gw = M_DIM // M_GROUPS
    heads_per_group = M_HEADS // M_GROUPS

    ys = []
    for g in range(M_GROUPS):
        b_g = bc[:, g * M_STATE:(g + 1) * M_STATE].astype(BF16)
        c_g = bc[:, (M_GROUPS + g) * M_STATE:(M_GROUPS + g + 1) * M_STATE].astype(BF16)
        gram = _dot_nt(c_g, b_g)
        st = s_ref[g]
        gsl = slice(g * gw, (g + 1) * gw)
        y_g = _dot(c_g, st.astype(BF16)) * e_cum_e[:, gsl]
        s_ref[g] = st * e_cum_e[c - 1:c, gsl] + _dot_tn(b_g, x_st[:, gsl])
        pieces = []
        for pr in range(heads_per_group // 2):
            psl = slice(g * gw + pr * LANES, g * gw + (pr + 1) * LANES)
            x_pair = x_dt[:, psl]
            acc = jnp.zeros((c, LANES), F32)
            for half in range(2):
                h = g * heads_per_group + 2 * pr + half
                seg = cum[:, h:h + 1] - cum_t[h:h + 1, :]
                decay = jnp.exp(jnp.where(causal, seg, -jnp.inf))
                in_head = (lane >= half * M_HEAD_DIM) & (lane < (half + 1) * M_HEAD_DIM)
                acc = acc + _dot((gram * decay).astype(BF16), jnp.where(in_head, x_pair, jnp.zeros_like(x_pair)))
            pieces.append(acc)
        ys.append(y_g + jnp.concatenate(pieces, axis=1))
    y = jnp.concatenate(ys, axis=1) + xs * dskip_ref[0]
    y = y * _silu(zx_ref[0, :, 0:M_DIM])
    outs = []
    for g in range(M_GROUPS):
        yg = y[:, g * gw:(g + 1) * gw]
        outs.append(yg * lax.rsqrt(jnp.mean(yg * yg, axis=-1, keepdims=True) + RMS_EPS))
    o_ref[0, :, RET_DIM + HG_DIM:] = (jnp.concatenate(outs, axis=1) * nw_ref[0]).astype(BF16)


def _mixers_kernel(layer, has_next, n_chunks, *refs):
    (cos_ref, sin_ref, rtab_ref, ret_ref, hg_ref, zx_ref, bc_ref, dt_ref, lbl_ref, hnw_ref, cw_ref, cb_ref, dtb_ref,
     alog_ref, dskip_ref, mnw_ref, tri_ref, expand_ref, wout_ref, w1_ref, w2_ref) = refs[:21]
    refs = refs[21:]
    if has_next:
        win_ref, refs = refs[0], refs[1:]
    o_ref, wout_o_ref, w1_o_ref, w2_o_ref = refs[:4]
    refs = refs[4:]
    if has_next:
        win_o_ref, refs = refs[0], refs[1:]
    ret_s_ref, hg_s_ref, m_s_ref, ext_x_ref, ext_b_ref = refs

    wout_o_ref[...] = wout_ref[...].astype(BF16)
    w1_o_ref[...] = w1_ref[...].astype(BF16)
    w2_o_ref[...] = w2_ref[...].astype(BF16)
    if has_next:
        _win_convert(win_ref, win_o_ref, pl.program_id(0) * n_chunks + pl.program_id(1))

    @pl.when(pl.program_id(1) == 0)
    def _():
        ret_s_ref[...] = jnp.zeros_like(ret_s_ref)
        hg_s_ref[...] = jnp.zeros_like(hg_s_ref)
        m_s_ref[...] = jnp.zeros_like(m_s_ref)
        ext_x_ref[0:SUBLANES, :] = jnp.zeros((SUBLANES, M_DIM), F32)
        ext_b_ref[0:SUBLANES, :] = jnp.zeros((SUBLANES, M_BC), F32)

    _retention_chunk(cos_ref, sin_ref, rtab_ref, ret_ref, o_ref, ret_s_ref)
    _hgrn2_chunk(layer, hg_ref, lbl_ref, hnw_ref, tri_ref, o_ref, hg_s_ref)
    _mamba2_chunk(zx_ref, bc_ref, dt_ref, cw_ref, cb_ref, dtb_ref, alog_ref, dskip_ref, mnw_ref, tri_ref, expand_ref,
                  o_ref, m_s_ref, ext_x_ref, ext_b_ref)


def _mixers(proj, cos2, sin2, ret_tabs, lb_logits, hg_norm_w3, conv_w, conv_b3, dtb3, alog3, dskip3, m_norm_w3, tri,
            expand, w_out, w1, w2, w_in_t, layer):
    bsz, seq, _ = proj.shape
    c = CHUNK
    n_chunks = seq // c
    n_steps = bsz * n_chunks
    has_next = layer + 1 < DEPTH

    def cols(first, width):
        assert first % width == 0
        return pl.BlockSpec((1, c, width), lambda b, i: (b, i, first // width))

    def lay(shape):
        return pl.BlockSpec((1,) + shape, lambda b, i: (layer, 0, 0))

    def const(shape):
        return pl.BlockSpec(shape, lambda b, i: (0,) * len(shape))

    def slab_in(w):
        rows = w.shape[1] // n_steps
        assert rows * n_steps == w.shape[1] and rows % (2 * SUBLANES) == 0
        return pl.BlockSpec((None, rows, w.shape[2]), lambda b, i: (layer, b * n_chunks + i, 0))

    def slab_out(w):
        rows = w.shape[1] // n_steps
        return (pl.BlockSpec((rows, w.shape[2]), lambda b, i: (b * n_chunks + i, 0)),
                jax.ShapeDtypeStruct(w.shape[1:], BF16))

    tab = pl.BlockSpec((1, c, HEAD_DIM), lambda b, i: (b, i, 0))
    in_specs = [
        tab, tab, const((RET_HEADS, 3, c, c)),
        cols(*COL_RET), cols(*COL_HG), cols(*COL_ZX), cols(*COL_BC), cols(*COL_DT),
        const((DEPTH, HG_DIM)), lay((1, HG_DIM)),
        lay((M_CONV, M_CONV_DIM)), lay((1, M_CONV_DIM)), lay((1, LANES)), lay((1, LANES)), lay((1, M_DIM)),
        lay((1, M_DIM)), const((c, c)), const((LANES, M_DIM)),
        slab_in(w_out), slab_in(w1), slab_in(w2),
    ]
    args = [cos2, sin2, ret_tabs, proj, proj, proj, proj, proj, lb_logits, hg_norm_w3, conv_w, conv_b3, dtb3, alog3,
            dskip3, m_norm_w3, tri, expand, w_out, w1, w2]
    out_specs = [pl.BlockSpec((1, c, D_MODEL), lambda b, i: (b, i, 0))]
    out_shape = [jax.ShapeDtypeStruct((bsz, seq, D_MODEL), BF16)]
    for w in (w_out, w1, w2):
        spec, shape = slab_out(w)
        out_specs.append(spec)
        out_shape.append(shape)
    if has_next:
        win_in, win_out, win_shape = _win_specs(layer + 1, n_steps, lambda b, i: b * n_chunks + i)
        in_specs.append(win_in)
        args.append(w_in_t)
        out_specs.append(win_out)
        out_shape.append(win_shape)
    return pl.pallas_call(
        functools.partial(_mixers_kernel, layer, has_next, n_chunks),
        grid=(bsz, n_chunks),
        in_specs=in_specs,
        out_specs=out_specs,
        out_shape=out_shape,
        scratch_shapes=[
            pltpu.VMEM((RET_HEADS, HEAD_DIM, HEAD_DIM), F32),
            pltpu.VMEM((HG_HEADS, HEAD_DIM, HEAD_DIM), F32),
            pltpu.VMEM((M_GROUPS, M_STATE, M_DIM // M_GROUPS), F32),
            pltpu.VMEM((SUBLANES + c, M_DIM), F32),
            pltpu.VMEM((SUBLANES + c, M_BC), F32),
        ],
        compiler_params=_params("arbitrary", "arbitrary"),
        name="mixers",
    )(*args)


def _residual_norm(x_ref, o_ref, gate, g, b):
    for r in range(0, o_ref.shape[1], LN_ROWS):
        rs = slice(r, r + LN_ROWS)
        o_ref[0, rs, :] = _layer_norm(DEEPNORM_ALPHA * x_ref[0, rs, :] + (1.0 + gate) * o_ref[0, rs, :], g, b)


def _outproj_kernel(x_ref, mix_ref, w_ref, ada_ref, g_ref, b_ref, o_ref):
    gate = ada_ref[0, 2:3, :]
    for r in range(0, o_ref.shape[1], OUTPROJ_TR):
        rs = slice(r, r + OUTPROJ_TR)
        y = _dot(mix_ref[0, rs, :], w_ref[...])
        o_ref[0, rs, :] = _layer_norm(DEEPNORM_ALPHA * x_ref[0, rs, :] + (1.0 + gate) * y,
                                      g_ref[0, 0:1, :], b_ref[0, 0:1, :])


def _outproj(x, mix, w_out_b, ada, ln_g, ln_b, layer):
    bsz, seq, _ = x.shape
    tm = min(OUTPROJ_TM, seq)
    rows = pl.BlockSpec((1, tm, D_MODEL), lambda b, m: (b, m, 0))
    return pl.pallas_call(
        _outproj_kernel,
        grid=(bsz, seq // tm),
        in_specs=[
            rows, rows,
            pl.BlockSpec((D_MODEL, D_MODEL), lambda b, m: (0, 0)),
            pl.BlockSpec((1, N_ADA, D_MODEL), lambda b, m: (layer * bsz + b, 0, 0)),
            pl.BlockSpec((1, 2, D_MODEL), lambda b, m: (layer, 0, 0)),
            pl.BlockSpec((1, 2, D_MODEL), lambda b, m: (layer, 0, 0)),
        ],
        out_specs=rows,
        out_shape=jax.ShapeDtypeStruct(x.shape, F32),
        compiler_params=_params("arbitrary", "arbitrary"),
        name="outproj",
    )(x, mix, w_out_b, ada, ln_g, ln_b)


def _mlp_kernel(x_ref, ada_ref, w1_ref, w2_ref, g_ref, b_ref, o_ref, h_ref):
    f = pl.program_id(2)

    @pl.when(f == 0)
    def _():
        sh = ada_ref[0, 3:4, :]
        sc = ada_ref[0, 4:5, :]
        h_ref[...] = (x_ref[0] * (1.0 + sc) + sh).astype(BF16)
        o_ref[...] = jnp.zeros_like(o_ref)

    halves = [slice(r, r + MLP_TR) for r in range(0, o_ref.shape[1], MLP_TR)]
    u2 = []
    for rs in halves:
        u = jnp.maximum(_dot(h_ref[rs, :], w1_ref[...]), 0.0)
        u2.append((u * u).astype(BF16))
    for rs, u2_r in zip(halves, u2):
        for n in range(0, D_MODEL, MLP_TN):
            o_ref[0, rs, n:n + MLP_TN] += _dot(u2_r, w2_ref[:, n:n + MLP_TN])

    @pl.when(f == pl.num_programs(2) - 1)
    def _():
        _residual_norm(x_ref, o_ref, ada_ref[0, 5:6, :], g_ref[0, 1:2, :], b_ref[0, 1:2, :])


def _mlp(x, ada, w1_b, w2_b, ln_g, ln_b, layer):
    bsz, seq, _ = x.shape
    tm = min(MLP_TM, seq)
    tf = MLP_TF
    return pl.pallas_call(
        _mlp_kernel,
        grid=(bsz, seq // tm, D_FF // tf),
        in_specs=[
            pl.BlockSpec((1, tm, D_MODEL), lambda b, m, f: (b, m, 0)),
            pl.BlockSpec((1, N_ADA, D_MODEL), lambda b, m, f: (layer * bsz + b, 0, 0)),
            pl.BlockSpec((D_MODEL, tf), lambda b, m, f: (0, f)),
            pl.BlockSpec((tf, D_MODEL), lambda b, m, f: (f, 0)),
            pl.BlockSpec((1, 2, D_MODEL), lambda b, m, f: (layer, 0, 0)),
            pl.BlockSpec((1, 2, D_MODEL), lambda b, m, f: (layer, 0, 0)),
        ],
        out_specs=pl.BlockSpec((1, tm, D_MODEL), lambda b, m, f: (b, m, 0)),
        out_shape=jax.ShapeDtypeStruct(x.shape, F32),
        scratch_shapes=[pltpu.VMEM((tm, D_MODEL), BF16)],
        compiler_params=_params("arbitrary", "arbitrary", "arbitrary"),
        name="mlp",
    )(x, ada, w1_b, w2_b, ln_g, ln_b)


def kernel(x, c, positions, lb_logits, w_in, w_out, w_ada, b_ada, ln_g, ln_b, hg_norm_w, m_conv_w, m_conv_b,
           m_dt_bias, m_a_log, m_d, m_norm_w, w1, w2):
    bsz, seq, _ = x.shape
    assert seq % CHUNK == 0

    pad_heads = ((0, 0), (0, LANES - M_HEADS))
    dtb3 = jnp.pad(m_dt_bias, pad_heads).reshape(DEPTH, 1, LANES)
    alog3 = jnp.pad(m_a_log, pad_heads).reshape(DEPTH, 1, LANES)
    dskip3 = jnp.repeat(m_d, M_HEAD_DIM, axis=1).reshape(DEPTH, 1, M_DIM)
    conv_b3 = m_conv_b.reshape(DEPTH, 1, M_CONV_DIM)
    m_norm_w3 = m_norm_w.reshape(DEPTH, 1, M_DIM)
    hg_norm_w3 = hg_norm_w.reshape(DEPTH, 1, HG_DIM)
    pos3 = positions.reshape(bsz, seq, 1)

    inv_freq = 1.0 / (ROPE_BASE ** jnp.linspace(0.0, 1.0, HEAD_DIM // 2, dtype=F32))
    invf2 = jnp.concatenate([inv_freq, inv_freq]).reshape(1, HEAD_DIM)
    idx = jnp.arange(CHUNK)
    tri = (idx[None, :] <= idx[:, None]).astype(BF16)
    expand = (jnp.arange(M_DIM)[None, :] // M_HEAD_DIM == jnp.arange(LANES)[:, None]).astype(BF16)
    ret_tabs = _retention_tables()

    w_in_t = jnp.swapaxes(w_in, 1, 2)
    ada = _ada(c, w_ada, b_ada).reshape(DEPTH * bsz, N_ADA, D_MODEL)
    cos2, sin2, w_in_b = _rope(pos3, invf2, w_in_t)
    for l in range(DEPTH):
        proj = _inproj(x, ada, w_in_b, l)
        mix, w_out_b, w1_b, w2_b, *rest = _mixers(proj, cos2, sin2, ret_tabs, lb_logits, hg_norm_w3, m_conv_w,
                                                  conv_b3, dtb3, alog3, dskip3, m_norm_w3, tri, expand, w_out, w1,
                                                  w2, w_in_t, l)
        if rest:
            w_in_b = rest[0]
        x = _outproj(x, mix, w_out_b, ada, ln_g, ln_b, l)
        x = _mlp(x, ada, w1_b, w2_b, ln_g, ln_b, l)
    return x
```

```python
import functools
import math

import jax
import jax.numpy as jnp
from jax import lax
from jax.experimental import pallas as pl
from jax.experimental.pallas import tpu as pltpu

F32 = jnp.float32
BF16 = jnp.bfloat16

D_MODEL = 2048
DEPTH = 4
RET_HEADS = 4
RET_DIM = 512
HEAD_DIM = 128
ROPE_BASE = 10000.0
HG_HEADS = 4
HG_DIM = 512
M_DIM = 1024
M_HEAD_DIM = 64
M_HEADS = 16
M_GROUPS = 2
M_STATE = 128
M_CONV = 4
M_CONV_DIM = 1536
M_BC = M_CONV_DIM - M_DIM
D_FF = 4 * D_MODEL
IN_COLS = 6672
DEEPNORM_ALPHA = (2 * DEPTH) ** 0.25
LN_EPS = 1e-5
RMS_EPS = 1e-6
N_ADA = 6
LOG2E = math.log2(math.e)

LANES = 128
SUBLANES = 8
IN_COLS_PAD = 6912
VMEM_LIMIT = 56 * 2 ** 20

COL_RET = (0, 4 * RET_DIM)
COL_HG = (4 * RET_DIM, 4 * HG_DIM)
COL_ZX = (4 * RET_DIM + 4 * HG_DIM, 2 * M_DIM)
COL_BC = (COL_ZX[0] + 2 * M_DIM, M_BC)
COL_DT = (COL_BC[0] + M_BC, LANES)

ADA_TN = 1024
ROPE_TM = 512
INPROJ_TM, INPROJ_TN = 256, 768
OUTPROJ_TM = 512
OUTPROJ_TR = 256
MLP_TM, MLP_TF = 1024, 512
MLP_TN = 512
MLP_TR = 512
LN_ROWS = 128

CHUNK = 128
HG_LEVELS = tuple(1 << i for i in range(int(math.log2(CHUNK))))


def _params(*sem):
    return pltpu.CompilerParams(dimension_semantics=sem, vmem_limit_bytes=VMEM_LIMIT)


def _silu(x):
    return x * jax.nn.sigmoid(x)


def _dot(a, b):
    return jnp.dot(a, b, preferred_element_type=F32)


def _dot_nt(a, b):
    return lax.dot_general(a, b, (((1,), (1,)), ((), ())), preferred_element_type=F32)


def _dot_tn(a, b):
    return lax.dot_general(a, b, (((0,), (0,)), ((), ())), preferred_element_type=F32)


def _split(x, pieces):
    out = []
    for _ in range(pieces - 1):
        p = x.astype(BF16)
        out.append(p)
        x = x - p.astype(F32)
    out.append(x.astype(BF16))
    return out


def _dot_split_rhs(m_bf16, x, pieces):
    return functools.reduce(jnp.add, [_dot(m_bf16, p) for p in _split(x, pieces)])


def _dot_split_lhs(x, m_bf16, pieces):
    return functools.reduce(jnp.add, [_dot(p, m_bf16) for p in _split(x, pieces)])


def _layer_norm(v, g, b):
    mu = jnp.mean(v, axis=-1, keepdims=True)
    d = v - mu
    var = jnp.mean(d * d, axis=-1, keepdims=True)
    return d * lax.rsqrt(var + LN_EPS) * g + b


def _win_slab_rows(n_steps):
    return LANES * pl.cdiv(IN_COLS_PAD, LANES * n_steps)


def _win_specs(layer, n_steps, step_of):
    rows = _win_slab_rows(n_steps)
    last_in = pl.cdiv(IN_COLS, rows) - 1
    last_out = pl.cdiv(IN_COLS_PAD, rows) - 1
    assert last_out < n_steps
    return (pl.BlockSpec((None, rows, D_MODEL), lambda *g: (layer, jnp.minimum(step_of(*g), last_in), 0)),
            pl.BlockSpec((rows, D_MODEL), lambda *g: (jnp.minimum(step_of(*g), last_out), 0)),
            jax.ShapeDtypeStruct(((last_out + 1) * rows, D_MODEL), BF16))


def _win_convert(win_ref, win_o_ref, step):
    rows = win_o_ref.shape[0]
    blk = jnp.minimum(step, pl.cdiv(IN_COLS_PAD, rows) - 1)
    n_idx = blk * rows + lax.broadcasted_iota(jnp.int32, (rows, 1), 0)
    win_o_ref[...] = jnp.where(n_idx < IN_COLS, win_ref[...], 0.0).astype(BF16)


def _ada_kernel(c_ref, w_ref, b_ref, o_ref):
    cond = _silu(c_ref[...]).astype(BF16)
    o_ref[0] = _dot(cond, w_ref[0].astype(BF16)) + b_ref[0]


def _ada(c, w_ada, b_ada):
    bsz = c.shape[0]
    tn = ADA_TN
    return pl.pallas_call(
        _ada_kernel,
        grid=(DEPTH, N_ADA * D_MODEL // tn),
        in_specs=[
            pl.BlockSpec((bsz, D_MODEL), lambda l, n: (0, 0)),
            pl.BlockSpec((1, D_MODEL, tn), lambda l, n: (l, 0, n)),
            pl.BlockSpec((1, 1, tn), lambda l, n: (l, 0, n)),
        ],
        out_specs=pl.BlockSpec((1, bsz, tn), lambda l, n: (l, 0, n)),
        out_shape=jax.ShapeDtypeStruct((DEPTH, bsz, N_ADA * D_MODEL), F32),
        compiler_params=_params("arbitrary", "arbitrary"),
        name="ada",
    )(c, w_ada, b_ada.reshape(DEPTH, 1, N_ADA * D_MODEL))


def _rope_kernel(n_tiles, pos_ref, invf_ref, win_ref, cos_ref, sin_ref, win_o_ref):
    _win_convert(win_ref, win_o_ref, pl.program_id(0) * n_tiles + pl.program_id(1))
    ang = pos_ref[0].astype(F32) * invf_ref[...]
    lane = lax.broadcasted_iota(jnp.int32, (1, HEAD_DIM), 1)
    cos_ref[0] = jnp.cos(ang)
    sin_ref[0] = jnp.where(lane < HEAD_DIM // 2, -1.0, 1.0) * jnp.sin(ang)


def _rope(pos3, invf2, w_in_t):
    bsz, seq, _ = pos3.shape
    tm = min(ROPE_TM, seq)
    n_tiles = seq // tm
    tab = pl.BlockSpec((1, tm, HEAD_DIM), lambda b, m: (b, m, 0))
    shape = jax.ShapeDtypeStruct((bsz, seq, HEAD_DIM), F32)
    win_in, win_out, win_shape = _win_specs(0, bsz * n_tiles, lambda b, m: b * n_tiles + m)
    return pl.pallas_call(
        functools.partial(_rope_kernel, n_tiles),
        grid=(bsz, n_tiles),
        in_specs=[pl.BlockSpec((1, tm, 1), lambda b, m: (b, m, 0)),
                  pl.BlockSpec((1, HEAD_DIM), lambda b, m: (0, 0)),
                  win_in],
        out_specs=(tab, tab, win_out),
        out_shape=(shape, shape, win_shape),
        compiler_params=_params("arbitrary", "arbitrary"),
        name="rope",
    )(pos3, invf2, w_in_t)


def _inproj_kernel(x_ref, ada_ref, wt_ref, o_ref):
    sh = ada_ref[0, 0:1, :]
    sc = ada_ref[0, 1:2, :]
    h = (x_ref[0] * (1.0 + sc) + sh).astype(BF16)
    for n in range(0, IN_COLS_PAD, INPROJ_TN):
        o_ref[0, :, n:n + INPROJ_TN] = _dot_nt(h, wt_ref[n:n + INPROJ_TN, :])


def _inproj(x, ada, w_in_t, layer):
    bsz, seq, _ = x.shape
    tm = min(INPROJ_TM, seq)
    return pl.pallas_call(
        _inproj_kernel,
        grid=(bsz, seq // tm),
        in_specs=[
            pl.BlockSpec((1, tm, D_MODEL), lambda b, m: (b, m, 0)),
            pl.BlockSpec((1, N_ADA, D_MODEL), lambda b, m: (layer * bsz + b, 0, 0)),
            pl.BlockSpec((IN_COLS_PAD, D_MODEL), lambda b, m: (0, 0), pipeline_mode=pl.Buffered(1)),
        ],
        out_specs=pl.BlockSpec((1, tm, IN_COLS_PAD), lambda b, m: (b, m, 0)),
        out_shape=jax.ShapeDtypeStruct((bsz, seq, IN_COLS_PAD), F32),
        compiler_params=_params("arbitrary", "arbitrary"),
        name="inproj",
    )(x, ada, w_in_t)


def _retention_tables():
    c = CHUNK
    i = jnp.arange(c, dtype=F32)[:, None]
    j = jnp.arange(c, dtype=F32)[None, :]
    ones = jnp.ones((1, HEAD_DIM), F32)
    tabs = []
    for h in range(RET_HEADS):
        log_gamma = math.log(1.0 - 2.0 ** (-5.0 - h))
        decay = jnp.where(i >= j, jnp.exp((i - j) * log_gamma), 0.0)
        tabs.append(jnp.stack([decay, jnp.exp((i + 1.0) * log_gamma) * ones, jnp.exp((c - 1.0 - i) * log_gamma) * ones]))
    return jnp.stack(tabs)


def _retention_chunk(cos_ref, sin_ref, tab_ref, p_ref, o_ref, s_ref):
    c = CHUNK
    cos2 = cos_ref[0]
    sin2 = sin_ref[0]

    def rot(t):
        return t * cos2 + pltpu.roll(t, HEAD_DIM // 2, 1) * sin2

    for h in range(RET_HEADS):
        log_gamma = math.log(1.0 - 2.0 ** (-5.0 - h))
        q = rot(p_ref[0, :, h * HEAD_DIM:(h + 1) * HEAD_DIM])
        k = rot(p_ref[0, :, RET_DIM + h * HEAD_DIM:RET_DIM + (h + 1) * HEAD_DIM]) * (HEAD_DIM ** -0.5)
        v = p_ref[0, :, 2 * RET_DIM + h * HEAD_DIM:2 * RET_DIM + (h + 1) * HEAD_DIM].astype(BF16)
        g = p_ref[0, :, 3 * RET_DIM + h * HEAD_DIM:3 * RET_DIM + (h + 1) * HEAD_DIM]
        scores = _dot_nt(q.astype(BF16), k.astype(BF16)) * tab_ref[h, 0]
        st = s_ref[h]
        y = _dot(scores.astype(BF16), v) + _dot((q * tab_ref[h, 1]).astype(BF16), st.astype(BF16))
        s_ref[h] = math.exp(c * log_gamma) * st + _dot_tn((k * tab_ref[h, 2]).astype(BF16), v)
        o = y * lax.rsqrt(jnp.mean(y * y, axis=-1, keepdims=True) + RMS_EPS)
        o_ref[0, :, h * HEAD_DIM:(h + 1) * HEAD_DIM] = (_silu(g) * o).astype(BF16)
        yield


def _hgrn2_chunk(layer, p_ref, lbl_ref, nw_ref, tri_ref, o_ref, s_ref):
    c = CHUNK
    lbl = lbl_ref[...]
    e = jnp.exp(lbl - jnp.max(lbl, axis=0, keepdims=True))
    p = e / jnp.sum(e, axis=0, keepdims=True)
    lb = jnp.zeros((1, HG_DIM), F32)
    for d in range(1, layer + 1):
        lb = lb + p[d:d + 1, :]

    q = p_ref[0, :, 0:HG_DIM]
    f = lb + (1.0 - lb) * jax.nn.sigmoid(p_ref[0, :, HG_DIM:2 * HG_DIM])
    log_f = jnp.log(f)
    k = 1.0 - f
    v_bf = p_ref[0, :, 2 * HG_DIM:3 * HG_DIM].astype(BF16)
    q_bf = q.astype(BF16)
    k_bf = k.astype(BF16)

    cum = _dot_split_rhs(tri_ref[...], log_f, 3)
    row = lax.broadcasted_iota(jnp.int32, (c, 1), 0)
    col = lax.broadcasted_iota(jnp.int32, (1, c), 1)
    r_sub = row & (SUBLANES - 1)
    yield

    def pick_row(x, block, j):
        x3 = x.reshape(c // block, block, HG_DIM)[:, j:j + 1, :]
        return jnp.broadcast_to(x3, (c // block, block, HG_DIM)).reshape(c, HG_DIM)

    merged, masks = [], []
    for b in HG_LEVELS:
        sh = int(math.log2(b))
        upper = ((row >> sh) & 1) == 1
        if b == 1:
            m = jnp.where(upper, q * f, k)
        else:
            if 2 * b >= SUBLANES:
                ref = pick_row(cum, 2 * b, b - 1)
            else:
                ref = jnp.where(r_sub < 2 * b, pick_row(cum, SUBLANES, b - 1), pick_row(cum, SUBLANES, 3 * b - 1))
            m = jnp.where(upper, q, k) * jnp.exp2((cum - ref) * jnp.where(upper, LOG2E, -LOG2E))
        merged.append(m.astype(BF16))
        masks.append(((row >> (sh + 1)) == (col >> (sh + 1))) & upper & (((col >> sh) & 1) == 0))
        yield

    tail = cum[c - 1:c, :] - cum
    q_in = (q * jnp.exp(cum)).astype(BF16)
    k_st = (k * jnp.exp(tail)).astype(BF16)
    e_tot = jnp.exp(cum[c - 1:c, :])

    outs = []
    for h in range(HG_HEADS):
        sl = slice(h * HEAD_DIM, (h + 1) * HEAD_DIM)
        scores = jnp.where(row == col, _dot_nt(q_bf[:, sl], k_bf[:, sl]), 0.0)
        for lv in range(len(HG_LEVELS)):
            m_h = merged[lv][:, sl]
            scores = jnp.where(masks[lv], _dot_nt(m_h, m_h), scores)
        st_t = s_ref[h]
        y = _dot(scores.astype(BF16), v_bf[:, sl]) + _dot_nt(q_in[:, sl], st_t.astype(BF16))
        s_ref[h] = st_t * e_tot[:, sl] + _dot_tn(v_bf[:, sl], k_st[:, sl])
        outs.append(y * lax.rsqrt(jnp.mean(y * y, axis=-1, keepdims=True) + RMS_EPS))
        yield
    o = jnp.concatenate(outs, axis=1) * nw_ref[0] * _silu(p_ref[0, :, 3 * HG_DIM:4 * HG_DIM])
    o_ref[0, :, RET_DIM:RET_DIM + HG_DIM] = o.astype(BF16)


def _mamba2_chunk(zx_ref, bc_ref, dt_ref, cw_ref, cb_ref, dtb_ref, alog_ref, dskip_ref, nw_ref, tri_ref, expand_ref,
                  o_ref, s_ref, ext_x_ref, ext_b_ref):
    c = CHUNK
    pad = SUBLANES

    def conv_silu(ext_ref, u, lo, hi):
        ext_ref[pad:pad + c, :] = u
        acc = cb_ref[0, :, lo:hi]
        for j in range(M_CONV):
            acc = acc + cw_ref[0, j:j + 1, lo:hi] * ext_ref[pad - (M_CONV - 1) + j:pad - (M_CONV - 1) + j + c, :]
        ext_ref[0:pad, :] = ext_ref[c:c + pad, :]
        return _silu(acc)

    xs = conv_silu(ext_x_ref, zx_ref[0, :, M_DIM:], 0, M_DIM)
    yield
    bc = conv_silu(ext_b_ref, bc_ref[0], M_DIM, M_CONV_DIM)
    yield

    dt = jax.nn.softplus(dt_ref[0] + dtb_ref[0])
    a = -jnp.exp(alog_ref[0])
    cum = _dot_split_rhs(tri_ref[...], dt * a, 3)
    cum_t = cum.T
    e_cum = jnp.exp(cum)
    w_state = dt * jnp.exp(cum[c - 1:c, :] - cum)
    spread = _dot_split_lhs(jnp.concatenate([dt, e_cum, w_state], axis=0), expand_ref[...], 2)
    dt_e, e_cum_e, w_state_e = spread[0:c], spread[c:2 * c], spread[2 * c:3 * c]
    x_dt = (xs * dt_e).astype(BF16)
    x_st = (xs * w_state_e).astype(BF16)
    yield

    row = lax.broadcasted_iota(jnp.int32, (c, 1), 0)
    col = lax.broadcasted_iota(jnp.int32, (1, c), 1)
    causal = row >= col
    lane = lax.broadcasted_iota(jnp.int32, (1, LANES), 1)
    gw = M_DIM // M_GROUPS
    heads_per_group = M_HEADS // M_GROUPS

    ys = []
    for g in range(M_GROUPS):
        b_g = bc[:, g * M_STATE:(g + 1) * M_STATE].astype(BF16)
        c_g = bc[:, (M_GROUPS + g) * M_STATE:(M_GROUPS + g + 1) * M_STATE].astype(BF16)
        gram = _dot_nt(c_g, b_g)
        st = s_ref[g]
        gsl = slice(g * gw, (g + 1) * gw)
        y_g = _dot(c_g, st.astype(BF16)) * e_cum_e[:, gsl]
        s_ref[g] = st * e_cum_e[c - 1:c, gsl] + _dot_tn(b_g, x_st[:, gsl])
        pieces = []
        for pr in range(heads_per_group // 2):
            psl = slice(g * gw + pr * LANES, g * gw + (pr + 1) * LANES)
            x_pair = x_dt[:, psl]
            acc = jnp.zeros((c, LANES), F32)
            for half in range(2):
                h = g * heads_per_group + 2 * pr + half
                seg = cum[:, h:h + 1] - cum_t[h:h + 1, :]
                decay = jnp.exp(jnp.where(causal, seg, -jnp.inf))
                in_head = (lane >= half * M_HEAD_DIM) & (lane < (half + 1) * M_HEAD_DIM)
                acc = acc + _dot((gram * decay).astype(BF16), jnp.where(in_head, x_pair, jnp.zeros_like(x_pair)))
            pieces.append(acc)
            yield
        ys.append(y_g + jnp.concatenate(pieces, axis=1))
    y = jnp.concatenate(ys, axis=1) + xs * dskip_ref[0]
    y = y * _silu(zx_ref[0, :, 0:M_DIM])
    outs = []
    for g in range(M_GROUPS):
        yg = y[:, g * gw:(g + 1) * gw]
        outs.append(yg * lax.rsqrt(jnp.mean(yg * yg, axis=-1, keepdims=True) + RMS_EPS))
    o_ref[0, :, RET_DIM + HG_DIM:] = (jnp.concatenate(outs, axis=1) * nw_ref[0]).astype(BF16)


def _mixers_kernel(layer, has_next, n_chunks, *refs):
    (cos_ref, sin_ref, rtab_ref, ret_ref, hg_ref, zx_ref, bc_ref, dt_ref, lbl_ref, hnw_ref, cw_ref, cb_ref, dtb_ref,
     alog_ref, dskip_ref, mnw_ref, tri_ref, expand_ref, wout_ref, w1_ref, w2_ref) = refs[:21]
    refs = refs[21:]
    if has_next:
        win_ref, refs = refs[0], refs[1:]
    o_ref, wout_o_ref, w1_o_ref, w2_o_ref = refs[:4]
    refs = refs[4:]
    if has_next:
        win_o_ref, refs = refs[0], refs[1:]
    ret_s_ref, hg_s_ref, m_s_ref, ext_x_ref, ext_b_ref = refs

    wout_o_ref[...] = wout_ref[...].astype(BF16)
    w1_o_ref[...] = w1_ref[...].astype(BF16)
    w2_o_ref[...] = w2_ref[...].astype(BF16)
    if has_next:
        _win_convert(win_ref, win_o_ref, pl.program_id(0) * n_chunks + pl.program_id(1))

    @pl.when(pl.program_id(1) == 0)
    def _():
        ret_s_ref[...] = jnp.zeros_like(ret_s_ref)
        hg_s_ref[...] = jnp.zeros_like(hg_s_ref)
        m_s_ref[...] = jnp.zeros_like(m_s_ref)
        ext_x_ref[0:SUBLANES, :] = jnp.zeros((SUBLANES, M_DIM), F32)
        ext_b_ref[0:SUBLANES, :] = jnp.zeros((SUBLANES, M_BC), F32)

    stages = [_hgrn2_chunk(layer, hg_ref, lbl_ref, hnw_ref, tri_ref, o_ref, hg_s_ref),
              _mamba2_chunk(zx_ref, bc_ref, dt_ref, cw_ref, cb_ref, dtb_ref, alog_ref, dskip_ref, mnw_ref, tri_ref,
                            expand_ref, o_ref, m_s_ref, ext_x_ref, ext_b_ref),
              _retention_chunk(cos_ref, sin_ref, rtab_ref, ret_ref, o_ref, ret_s_ref)]
    while stages:
        for stage in list(stages):
            if next(stage, "done") == "done":
                stages.remove(stage)


def _mixers(proj, cos2, sin2, ret_tabs, lb_logits, hg_norm_w3, conv_w, conv_b3, dtb3, alog3, dskip3, m_norm_w3, tri,
            expand, w_out, w1, w2, w_in_t, layer):
    bsz, seq, _ = proj.shape
    c = CHUNK
    n_chunks = seq // c
    n_steps = bsz * n_chunks
    has_next = layer + 1 < DEPTH

    def cols(first, width):
        assert first % width == 0
        return pl.BlockSpec((1, c, width), lambda b, i: (b, i, first // width))

    def lay(shape):
        return pl.BlockSpec((1,) + shape, lambda b, i: (layer, 0, 0))

    def const(shape):
        return pl.BlockSpec(shape, lambda b, i: (0,) * len(shape))

    def slab_in(w):
        rows = w.shape[1] // n_steps
        assert rows * n_steps == w.shape[1] and rows % (2 * SUBLANES) == 0
        return pl.BlockSpec((None, rows, w.shape[2]), lambda b, i: (layer, b * n_chunks + i, 0))

    def slab_out(w):
        rows = w.shape[1] // n_steps
        return (pl.BlockSpec((rows, w.shape[2]), lambda b, i: (b * n_chunks + i, 0)),
                jax.ShapeDtypeStruct(w.shape[1:], BF16))

    tab = pl.BlockSpec((1, c, HEAD_DIM), lambda b, i: (b, i, 0))
    in_specs = [
        tab, tab, const((RET_HEADS, 3, c, c)),
        cols(*COL_RET), cols(*COL_HG), cols(*COL_ZX), cols(*COL_BC), cols(*COL_DT),
        const((DEPTH, HG_DIM)), lay((1, HG_DIM)),
        lay((M_CONV, M_CONV_DIM)), lay((1, M_CONV_DIM)), lay((1, LANES)), lay((1, LANES)), lay((1, M_DIM)),
        lay((1, M_DIM)), const((c, c)), const((LANES, M_DIM)),
        slab_in(w_out), slab_in(w1), slab_in(w2),
    ]
    args = [cos2, sin2, ret_tabs, proj, proj, proj, proj, proj, lb_logits, hg_norm_w3, conv_w, conv_b3, dtb3, alog3,
            dskip3, m_norm_w3, tri, expand, w_out, w1, w2]
    out_specs = [pl.BlockSpec((1, c, D_MODEL), lambda b, i: (b, i, 0))]
    out_shape = [jax.ShapeDtypeStruct((bsz, seq, D_MODEL), BF16)]
    for w in (w_out, w1, w2):
        spec, shape = slab_out(w)
        out_specs.append(spec)
        out_shape.append(shape)
    if has_next:
        win_in, win_out, win_shape = _win_specs(layer + 1, n_steps, lambda b, i: b * n_chunks + i)
        in_specs.append(win_in)
        args.append(w_in_t)
        out_specs.append(win_out)
        out_shape.append(win_shape)
    return pl.pallas_call(
        functools.partial(_mixers_kernel, layer, has_next, n_chunks),
        grid=(bsz, n_chunks),
        in_specs=in_specs,
        out_specs=out_specs,
        out_shape=out_shape,
        scratch_shapes=[
            pltpu.VMEM((RET_HEADS, HEAD_DIM, HEAD_DIM), F32),
            pltpu.VMEM((HG_HEADS, HEAD_DIM, HEAD_DIM), F32),
            pltpu.VMEM((M_GROUPS, M_STATE, M_DIM // M_GROUPS), F32),
            pltpu.VMEM((SUBLANES + c, M_DIM), F32),
            pltpu.VMEM((SUBLANES + c, M_BC), F32),
        ],
        compiler_params=_params("arbitrary", "arbitrary"),
        name="mixers",
    )(*args)


def _residual_norm(x_ref, o_ref, gate, g, b):
    for r in range(0, o_ref.shape[1], LN_ROWS):
        rs = slice(r, r + LN_ROWS)
        o_ref[0, rs, :] = _layer_norm(DEEPNORM_ALPHA * x_ref[0, rs, :] + (1.0 + gate) * o_ref[0, rs, :], g, b)


def _outproj_kernel(x_ref, mix_ref, w_ref, ada_ref, g_ref, b_ref, o_ref):
    gate = ada_ref[0, 2:3, :]
    for r in range(0, o_ref.shape[1], OUTPROJ_TR):
        rs = slice(r, r + OUTPROJ_TR)
        y = _dot(mix_ref[0, rs, :], w_ref[...])
        o_ref[0, rs, :] = _layer_norm(DEEPNORM_ALPHA * x_ref[0, rs, :] + (1.0 + gate) * y,
                                      g_ref[0, 0:1, :], b_ref[0, 0:1, :])


def _outproj(x, mix, w_out_b, ada, ln_g, ln_b, layer):
    bsz, seq, _ = x.shape
    tm = min(OUTPROJ_TM, seq)
    rows = pl.BlockSpec((1, tm, D_MODEL), lambda b, m: (b, m, 0))
    return pl.pallas_call(
        _outproj_kernel,
        grid=(bsz, seq // tm),
        in_specs=[
            rows, rows,
            pl.BlockSpec((D_MODEL, D_MODEL), lambda b, m: (0, 0)),
            pl.BlockSpec((1, N_ADA, D_MODEL), lambda b, m: (layer * bsz + b, 0, 0)),
            pl.BlockSpec((1, 2, D_MODEL), lambda b, m: (layer, 0, 0)),
            pl.BlockSpec((1, 2, D_MODEL), lambda b, m: (layer, 0, 0)),
        ],
        out_specs=rows,
        out_shape=jax.ShapeDtypeStruct(x.shape, F32),
        compiler_params=_params("arbitrary", "arbitrary"),
        name="outproj",
    )(x, mix, w_out_b, ada, ln_g, ln_b)


def _mlp_kernel(x_ref, ada_ref, w1_ref, w2_ref, g_ref, b_ref, o_ref, h_ref):
    f = pl.program_id(2)

    @pl.when(f == 0)
    def _():
        sh = ada_ref[0, 3:4, :]
        sc = ada_ref[0, 4:5, :]
        h_ref[...] = (x_ref[0] * (1.0 + sc) + sh).astype(BF16)
        o_ref[...] = jnp.zeros_like(o_ref)

    halves = [slice(r, r + MLP_TR) for r in range(0, o_ref.shape[1], MLP_TR)]
    u2 = []
    for rs in halves:
        u = jnp.maximum(_dot(h_ref[rs, :], w1_ref[...]), 0.0)
        u2.append((u * u).astype(BF16))
    for rs, u2_r in zip(halves, u2):
        for n in range(0, D_MODEL, MLP_TN):
            o_ref[0, rs, n:n + MLP_TN] += _dot(u2_r, w2_ref[:, n:n + MLP_TN])

    @pl.when(f == pl.num_programs(2) - 1)
    def _():
        _residual_norm(x_ref, o_ref, ada_ref[0, 5:6, :], g_ref[0, 1:2, :], b_ref[0, 1:2, :])


def _mlp(x, ada, w1_b, w2_b, ln_g, ln_b, layer):
    bsz, seq, _ = x.shape
    tm = min(MLP_TM, seq)
    tf = MLP_TF
    return pl.pallas_call(
        _mlp_kernel,
        grid=(bsz, seq // tm, D_FF // tf),
        in_specs=[
            pl.BlockSpec((1, tm, D_MODEL), lambda b, m, f: (b, m, 0)),
            pl.BlockSpec((1, N_ADA, D_MODEL), lambda b, m, f: (layer * bsz + b, 0, 0)),
            pl.BlockSpec((D_MODEL, tf), lambda b, m, f: (0, f)),
            pl.BlockSpec((tf, D_MODEL), lambda b, m, f: (f, 0)),
            pl.BlockSpec((1, 2, D_MODEL), lambda b, m, f: (layer, 0, 0)),
            pl.BlockSpec((1, 2, D_MODEL), lambda b, m, f: (layer, 0, 0)),
        ],
        out_specs=pl.BlockSpec((1, tm, D_MODEL), lambda b, m, f: (b, m, 0)),
        out_shape=jax.ShapeDtypeStruct(x.shape, F32),
        scratch_shapes=[pltpu.VMEM((tm, D_MODEL), BF16)],
        compiler_params=_params("arbitrary", "arbitrary", "arbitrary"),
        name="mlp",
    )(x, ada, w1_b, w2_b, ln_g, ln_b)


def kernel(x, c, positions, lb_logits, w_in, w_out, w_ada, b_ada, ln_g, ln_b, hg_norm_w, m_conv_w, m_conv_b,
           m_dt_bias, m_a_log, m_d, m_norm_w, w1, w2):
    bsz, seq, _ = x.shape
    assert seq % CHUNK == 0

    pad_heads = ((0, 0), (0, LANES - M_HEADS))
    dtb3 = jnp.pad(m_dt_bias, pad_heads).reshape(DEPTH, 1, LANES)
    alog3 = jnp.pad(m_a_log, pad_heads).reshape(DEPTH, 1, LANES)
    dskip3 = jnp.repeat(m_d, M_HEAD_DIM, axis=1).reshape(DEPTH, 1, M_DIM)
    conv_b3 = m_conv_b.reshape(DEPTH, 1, M_CONV_DIM)
    m_norm_w3 = m_norm_w.reshape(DEPTH, 1, M_DIM)
    hg_norm_w3 = hg_norm_w.reshape(DEPTH, 1, HG_DIM)
    pos3 = positions.reshape(bsz, seq, 1)

    inv_freq = 1.0 / (ROPE_BASE ** jnp.linspace(0.0, 1.0, HEAD_DIM // 2, dtype=F32))
    invf2 = jnp.concatenate([inv_freq, inv_freq]).reshape(1, HEAD_DIM)
    idx = jnp.arange(CHUNK)
    tri = (idx[None, :] <= idx[:, None]).astype(BF16)
    expand = (jnp.arange(M_DIM)[None, :] // M_HEAD_DIM == jnp.arange(LANES)[:, None]).astype(BF16)
    ret_tabs = _retention_tables()

    w_in_t = jnp.swapaxes(w_in, 1, 2)
    ada = _ada(c, w_ada, b_ada).reshape(DEPTH * bsz, N_ADA, D_MODEL)
    cos2, sin2, w_in_b = _rope(pos3, invf2, w_in_t)
    for l in range(DEPTH):
        proj = _inproj(x, ada, w_in_b, l)
        mix, w_out_b, w1_b, w2_b, *rest = _mixers(proj, cos2, sin2, ret_tabs, lb_logits, hg_norm_w3, m_conv_w,
                                                  conv_b3, dtb3, alog3, dskip3, m_norm_w3, tri, expand, w_out, w1,
                                                  w2, w_in_t, l)
        if rest:
            w_in_b = rest[0]
        x = _outproj(x, mix, w_out_b, ada, ln_g, ln_b, l)
        x = _mlp(x, ada, w1_b, w2_b, ln_g, ln_b, l)
    return x
```

```python
import functools
import math

import jax
import jax.numpy as jnp
from jax import lax
from jax.experimental import pallas as pl
from jax.experimental.pallas import tpu as pltpu

F32 = jnp.float32
BF16 = jnp.bfloat16

D_MODEL = 2048
DEPTH = 4
RET_HEADS = 4
RET_DIM = 512
HEAD_DIM = 128
ROPE_BASE = 10000.0
HG_HEADS = 4
HG_DIM = 512
M_DIM = 1024
M_HEAD_DIM = 64
M_HEADS = 16
M_GROUPS = 2
M_STATE = 128
M_CONV = 4
M_CONV_DIM = 1536
M_BC = M_CONV_DIM - M_DIM
D_FF = 4 * D_MODEL
IN_COLS = 6672
DEEPNORM_ALPHA = (2 * DEPTH) ** 0.25
LN_EPS = 1e-5
RMS_EPS = 1e-6
N_ADA = 6
LOG2E = math.log2(math.e)

LANES = 128
SUBLANES = 8
IN_COLS_PAD = 6912
VMEM_LIMIT = 56 * 2 ** 20

COL_RET = (0, 4 * RET_DIM)
COL_HG = (4 * RET_DIM, 4 * HG_DIM)
COL_ZX = (4 * RET_DIM + 4 * HG_DIM, 2 * M_DIM)
COL_BC = (COL_ZX[0] + 2 * M_DIM, M_BC)
COL_DT = (COL_BC[0] + M_BC, LANES)

ADA_TN = 1024
ROPE_TM = 512
INPROJ_TM, INPROJ_TN = 256, 768
OUTPROJ_TM = 512
OUTPROJ_TR = 256
MLP_TM, MLP_TF = 1024, 512
MLP_TN = 512
MLP_TR = 256
LN_ROWS = 128

CHUNK = 128
HG_LEVELS = tuple(1 << i for i in range(int(math.log2(CHUNK))))


def _params(*sem):
    return pltpu.CompilerParams(dimension_semantics=sem, vmem_limit_bytes=VMEM_LIMIT)


def _silu(x):
    return x * jax.nn.sigmoid(x)


def _dot(a, b):
    return jnp.dot(a, b, preferred_element_type=F32)


def _dot_nt(a, b):
    return lax.dot_general(a, b, (((1,), (1,)), ((), ())), preferred_element_type=F32)


def _dot_tn(a, b):
    return lax.dot_general(a, b, (((0,), (0,)), ((), ())), preferred_element_type=F32)


def _split(x, pieces):
    out = []
    for _ in range(pieces - 1):
        p = x.astype(BF16)
        out.append(p)
        x = x - p.astype(F32)
    out.append(x.astype(BF16))
    return out


def _dot_split_rhs(m_bf16, x, pieces):
    return functools.reduce(jnp.add, [_dot(m_bf16, p) for p in _split(x, pieces)])


def _dot_split_lhs(x, m_bf16, pieces):
    return functools.reduce(jnp.add, [_dot(p, m_bf16) for p in _split(x, pieces)])


def _layer_norm(v, g, b):
    mu = jnp.mean(v, axis=-1, keepdims=True)
    d = v - mu
    var = jnp.mean(d * d, axis=-1, keepdims=True)
    return d * lax.rsqrt(var + LN_EPS) * g + b


def _win_slab_rows(n_steps):
    return LANES * pl.cdiv(IN_COLS_PAD, LANES * n_steps)


def _win_specs(layer, n_steps, step_of):
    rows = _win_slab_rows(n_steps)
    last_in = pl.cdiv(IN_COLS, rows) - 1
    last_out = pl.cdiv(IN_COLS_PAD, rows) - 1
    assert last_out < n_steps
    return (pl.BlockSpec((None, rows, D_MODEL), lambda *g: (layer, jnp.minimum(step_of(*g), last_in), 0)),
            pl.BlockSpec((rows, D_MODEL), lambda *g: (jnp.minimum(step_of(*g), last_out), 0)),
            jax.ShapeDtypeStruct(((last_out + 1) * rows, D_MODEL), BF16))


def _win_convert(win_ref, win_o_ref, step):
    rows = win_o_ref.shape[0]
    blk = jnp.minimum(step, pl.cdiv(IN_COLS_PAD, rows) - 1)
    n_idx = blk * rows + lax.broadcasted_iota(jnp.int32, (rows, 1), 0)
    win_o_ref[...] = jnp.where(n_idx < IN_COLS, win_ref[...], 0.0).astype(BF16)


def _ada_kernel(c_ref, w_ref, b_ref, o_ref):
    cond = _silu(c_ref[...]).astype(BF16)
    o_ref[0] = _dot(cond, w_ref[0].astype(BF16)) + b_ref[0]


def _ada(c, w_ada, b_ada):
    bsz = c.shape[0]
    tn = ADA_TN
    return pl.pallas_call(
        _ada_kernel,
        grid=(DEPTH, N_ADA * D_MODEL // tn),
        in_specs=[
            pl.BlockSpec((bsz, D_MODEL), lambda l, n: (0, 0)),
            pl.BlockSpec((1, D_MODEL, tn), lambda l, n: (l, 0, n)),
            pl.BlockSpec((1, 1, tn), lambda l, n: (l, 0, n)),
        ],
        out_specs=pl.BlockSpec((1, bsz, tn), lambda l, n: (l, 0, n)),
        out_shape=jax.ShapeDtypeStruct((DEPTH, bsz, N_ADA * D_MODEL), F32),
        compiler_params=_params("arbitrary", "arbitrary"),
        name="ada",
    )(c, w_ada, b_ada.reshape(DEPTH, 1, N_ADA * D_MODEL))


def _rope_kernel(n_tiles, pos_ref, invf_ref, win_ref, cos_ref, sin_ref, win_o_ref):
    _win_convert(win_ref, win_o_ref, pl.program_id(0) * n_tiles + pl.program_id(1))
    ang = pos_ref[0].astype(F32) * invf_ref[...]
    lane = lax.broadcasted_iota(jnp.int32, (1, HEAD_DIM), 1)
    cos_ref[0] = jnp.cos(ang)
    sin_ref[0] = jnp.where(lane < HEAD_DIM // 2, -1.0, 1.0) * jnp.sin(ang)


def _rope(pos3, invf2, w_in_t):
    bsz, seq, _ = pos3.shape
    tm = min(ROPE_TM, seq)
    n_tiles = seq // tm
    tab = pl.BlockSpec((1, tm, HEAD_DIM), lambda b, m: (b, m, 0))
    shape = jax.ShapeDtypeStruct((bsz, seq, HEAD_DIM), F32)
    win_in, win_out, win_shape = _win_specs(0, bsz * n_tiles, lambda b, m: b * n_tiles + m)
    return pl.pallas_call(
        functools.partial(_rope_kernel, n_tiles),
        grid=(bsz, n_tiles),
        in_specs=[pl.BlockSpec((1, tm, 1), lambda b, m: (b, m, 0)),
                  pl.BlockSpec((1, HEAD_DIM), lambda b, m: (0, 0)),
                  win_in],
        out_specs=(tab, tab, win_out),
        out_shape=(shape, shape, win_shape),
        compiler_params=_params("arbitrary", "arbitrary"),
        name="rope",
    )(pos3, invf2, w_in_t)


def _inproj_kernel(x_ref, ada_ref, wt_ref, o_ref):
    sh = ada_ref[0, 0:1, :]
    sc = ada_ref[0, 1:2, :]
    h = (x_ref[0] * (1.0 + sc) + sh).astype(BF16)
    for n in range(0, IN_COLS_PAD, INPROJ_TN):
        o_ref[0, :, n:n + INPROJ_TN] = _dot_nt(h, wt_ref[n:n + INPROJ_TN, :])


def _inproj(x, ada, w_in_t, layer):
    bsz, seq, _ = x.shape
    tm = min(INPROJ_TM, seq)
    return pl.pallas_call(
        _inproj_kernel,
        grid=(bsz, seq // tm),
        in_specs=[
            pl.BlockSpec((1, tm, D_MODEL), lambda b, m: (b, m, 0)),
            pl.BlockSpec((1, N_ADA, D_MODEL), lambda b, m: (layer * bsz + b, 0, 0)),
            pl.BlockSpec((IN_COLS_PAD, D_MODEL), lambda b, m: (0, 0), pipeline_mode=pl.Buffered(1)),
        ],
        out_specs=pl.BlockSpec((1, tm, IN_COLS_PAD), lambda b, m: (b, m, 0)),
        out_shape=jax.ShapeDtypeStruct((bsz, seq, IN_COLS_PAD), F32),
        compiler_params=_params("arbitrary", "arbitrary"),
        name="inproj",
    )(x, ada, w_in_t)


def _retention_tables():
    c = CHUNK
    i = jnp.arange(c, dtype=F32)[:, None]
    j = jnp.arange(c, dtype=F32)[None, :]
    ones = jnp.ones((1, HEAD_DIM), F32)
    tabs = []
    for h in range(RET_HEADS):
        log_gamma = math.log(1.0 - 2.0 ** (-5.0 - h))
        decay = jnp.where(i >= j, jnp.exp((i - j) * log_gamma), 0.0)
        tabs.append(jnp.stack([decay, jnp.exp((i + 1.0) * log_gamma) * ones, jnp.exp((c - 1.0 - i) * log_gamma) * ones]))
    return jnp.stack(tabs)


def _retention_chunk(cos_ref, sin_ref, tab_ref, p_ref, o_ref, s_ref):
    c = CHUNK
    cos2 = cos_ref[0]
    sin2 = sin_ref[0]

    def rot(t):
        return t * cos2 + pltpu.roll(t, HEAD_DIM // 2, 1) * sin2

    for h in range(RET_HEADS):
        log_gamma = math.log(1.0 - 2.0 ** (-5.0 - h))
        q = rot(p_ref[0, :, h * HEAD_DIM:(h + 1) * HEAD_DIM])
        k = rot(p_ref[0, :, RET_DIM + h * HEAD_DIM:RET_DIM + (h + 1) * HEAD_DIM]) * (HEAD_DIM ** -0.5)
        v = p_ref[0, :, 2 * RET_DIM + h * HEAD_DIM:2 * RET_DIM + (h + 1) * HEAD_DIM].astype(BF16)
        g = p_ref[0, :, 3 * RET_DIM + h * HEAD_DIM:3 * RET_DIM + (h + 1) * HEAD_DIM]
        scores = _dot_nt(q.astype(BF16), k.astype(BF16)) * tab_ref[h, 0]
        st = s_ref[h]
        y = _dot(scores.astype(BF16), v) + _dot((q * tab_ref[h, 1]).astype(BF16), st.astype(BF16))
        s_ref[h] = math.exp(c * log_gamma) * st + _dot_tn((k * tab_ref[h, 2]).astype(BF16), v)
        o = y * lax.rsqrt(jnp.mean(y * y, axis=-1, keepdims=True) + RMS_EPS)
        o_ref[0, :, h * HEAD_DIM:(h + 1) * HEAD_DIM] = (_silu(g) * o).astype(BF16)
        yield


def _hgrn2_chunk(layer, p_ref, lbl_ref, nw_ref, tri_ref, o_ref, s_ref):
    c = CHUNK
    lbl = lbl_ref[...]
    e = jnp.exp(lbl - jnp.max(lbl, axis=0, keepdims=True))
    p = e / jnp.sum(e, axis=0, keepdims=True)
    lb = jnp.zeros((1, HG_DIM), F32)
    for d in range(1, layer + 1):
        lb = lb + p[d:d + 1, :]

    q = p_ref[0, :, 0:HG_DIM]
    f = lb + (1.0 - lb) * jax.nn.sigmoid(p_ref[0, :, HG_DIM:2 * HG_DIM])
    log_f = jnp.log(f)
    k = 1.0 - f
    v_bf = p_ref[0, :, 2 * HG_DIM:3 * HG_DIM].astype(BF16)
    q_bf = q.astype(BF16)
    k_bf = k.astype(BF16)

    cum = _dot_split_rhs(tri_ref[...], log_f, 3)
    row = lax.broadcasted_iota(jnp.int32, (c, 1), 0)
    col = lax.broadcasted_iota(jnp.int32, (1, c), 1)
    r_sub = row & (SUBLANES - 1)
    yield

    def pick_row(x, block, j):
        x3 = x.reshape(c // block, block, HG_DIM)[:, j:j + 1, :]
        return jnp.broadcast_to(x3, (c // block, block, HG_DIM)).reshape(c, HG_DIM)

    merged, masks = [], []
    for b in HG_LEVELS:
        sh = int(math.log2(b))
        upper = ((row >> sh) & 1) == 1
        if b == 1:
            m = jnp.where(upper, q * f, k)
        else:
            if 2 * b >= SUBLANES:
                ref = pick_row(cum, 2 * b, b - 1)
            else:
                ref = jnp.where(r_sub < 2 * b, pick_row(cum, SUBLANES, b - 1), pick_row(cum, SUBLANES, 3 * b - 1))
            m = jnp.where(upper, q, k) * jnp.exp2((cum - ref) * jnp.where(upper, LOG2E, -LOG2E))
        merged.append(m.astype(BF16))
        masks.append(((row >> (sh + 1)) == (col >> (sh + 1))) & upper & (((col >> sh) & 1) == 0))
        yield

    tail = cum[c - 1:c, :] - cum
    q_in = (q * jnp.exp(cum)).astype(BF16)
    k_st = (k * jnp.exp(tail)).astype(BF16)
    e_tot = jnp.exp(cum[c - 1:c, :])

    outs = []
    for h in range(HG_HEADS):
        sl = slice(h * HEAD_DIM, (h + 1) * HEAD_DIM)
        scores = jnp.where(row == col, _dot_nt(q_bf[:, sl], k_bf[:, sl]), 0.0)
        for lv in range(len(HG_LEVELS)):
            m_h = merged[lv][:, sl]
            scores = jnp.where(masks[lv], _dot_nt(m_h, m_h), scores)
        st_t = s_ref[h]
        y = _dot(scores.astype(BF16), v_bf[:, sl]) + _dot_nt(q_in[:, sl], st_t.astype(BF16))
        s_ref[h] = st_t * e_tot[:, sl] + _dot_tn(v_bf[:, sl], k_st[:, sl])
        outs.append(y * lax.rsqrt(jnp.mean(y * y, axis=-1, keepdims=True) + RMS_EPS))
        yield
    o = jnp.concatenate(outs, axis=1) * nw_ref[0] * _silu(p_ref[0, :, 3 * HG_DIM:4 * HG_DIM])
    o_ref[0, :, RET_DIM:RET_DIM + HG_DIM] = o.astype(BF16)


def _mamba2_chunk(zx_ref, bc_ref, dt_ref, cw_ref, cb_ref, dtb_ref, alog_ref, dskip_ref, nw_ref, tri_ref, expand_ref,
                  o_ref, s_ref, ext_x_ref, ext_b_ref):
    c = CHUNK
    pad = SUBLANES

    def conv_silu(ext_ref, u, lo, hi):
        ext_ref[pad:pad + c, :] = u
        acc = cb_ref[0, :, lo:hi]
        for j in range(M_CONV):
            acc = acc + cw_ref[0, j:j + 1, lo:hi] * ext_ref[pad - (M_CONV - 1) + j:pad - (M_CONV - 1) + j + c, :]
        ext_ref[0:pad, :] = ext_ref[c:c + pad, :]
        return _silu(acc)

    xs = conv_silu(ext_x_ref, zx_ref[0, :, M_DIM:], 0, M_DIM)
    yield
    bc = conv_silu(ext_b_ref, bc_ref[0], M_DIM, M_CONV_DIM)
    yield

    dt = jax.nn.softplus(dt_ref[0] + dtb_ref[0])
    a = -jnp.exp(alog_ref[0])
    cum = _dot_split_rhs(tri_ref[...], dt * a, 3)
    cum_t = cum.T
    e_cum = jnp.exp(cum)
    w_state = dt * jnp.exp(cum[c - 1:c, :] - cum)
    spread = _dot_split_lhs(jnp.concatenate([dt, e_cum, w_state], axis=0), expand_ref[...], 2)
    dt_e, e_cum_e, w_state_e = spread[0:c], spread[c:2 * c], spread[2 * c:3 * c]
    x_dt = (xs * dt_e).astype(BF16)
    x_st = (xs * w_state_e).astype(BF16)
    yield

    row = lax.broadcasted_iota(jnp.int32, (c, 1), 0)
    col = lax.broadcasted_iota(jnp.int32, (1, c), 1)
    causal = row >= col
    lane = lax.broadcasted_iota(jnp.int32, (1, LANES), 1)
    gw = M_DIM // M_GROUPS
    heads_per_group = M_HEADS // M_GROUPS

    ys = []
    for g in range(M_GROUPS):
        b_g = bc[:, g * M_STATE:(g + 1) * M_STATE].astype(BF16)
        c_g = bc[:, (M_GROUPS + g) * M_STATE:(M_GROUPS + g + 1) * M_STATE].astype(BF16)
        gram = _dot_nt(c_g, b_g)
        st = s_ref[g]
        gsl = slice(g * gw, (g + 1) * gw)
        y_g = _dot(c_g, st.astype(BF16)) * e_cum_e[:, gsl]
        s_ref[g] = st * e_cum_e[c - 1:c, gsl] + _dot_tn(b_g, x_st[:, gsl])
        pieces = []
        for pr in range(heads_per_group // 2):
            psl = slice(g * gw + pr * LANES, g * gw + (pr + 1) * LANES)
            x_pair = x_dt[:, psl]
            acc = jnp.zeros((c, LANES), F32)
            for half in range(2):
                h = g * heads_per_group + 2 * pr + half
                seg = cum[:, h:h + 1] - cum_t[h:h + 1, :]
                decay = jnp.exp(jnp.where(causal, seg, -jnp.inf))
                in_head = (lane >= half * M_HEAD_DIM) & (lane < (half + 1) * M_HEAD_DIM)
                acc = acc + _dot((gram * decay).astype(BF16), jnp.where(in_head, x_pair, jnp.zeros_like(x_pair)))
            pieces.append(acc)
            yield
        ys.append(y_g + jnp.concatenate(pieces, axis=1))
    y = jnp.concatenate(ys, axis=1) + xs * dskip_ref[0]
    y = y * _silu(zx_ref[0, :, 0:M_DIM])
    outs = []
    for g in range(M_GROUPS):
        yg = y[:, g * gw:(g + 1) * gw]
        outs.append(yg * lax.rsqrt(jnp.mean(yg * yg, axis=-1, keepdims=True) + RMS_EPS))
    o_ref[0, :, RET_DIM + HG_DIM:] = (jnp.concatenate(outs, axis=1) * nw_ref[0]).astype(BF16)


def _mixers_kernel(layer, has_next, n_chunks, *refs):
    (cos_ref, sin_ref, rtab_ref, ret_ref, hg_ref, zx_ref, bc_ref, dt_ref, lbl_ref, hnw_ref, cw_ref, cb_ref, dtb_ref,
     alog_ref, dskip_ref, mnw_ref, tri_ref, expand_ref, wout_ref, w1_ref, w2_ref) = refs[:21]
    refs = refs[21:]
    if has_next:
        win_ref, refs = refs[0], refs[1:]
    o_ref, wout_o_ref, w1_o_ref, w2_o_ref = refs[:4]
    refs = refs[4:]
    if has_next:
        win_o_ref, refs = refs[0], refs[1:]
    ret_s_ref, hg_s_ref, m_s_ref, ext_x_ref, ext_b_ref = refs

    wout_o_ref[...] = wout_ref[...].astype(BF16)
    w1_o_ref[...] = w1_ref[...].astype(BF16)
    w2_o_ref[...] = w2_ref[...].astype(BF16)
    if has_next:
        _win_convert(win_ref, win_o_ref, pl.program_id(0) * n_chunks + pl.program_id(1))

    @pl.when(pl.program_id(1) == 0)
    def _():
        ret_s_ref[...] = jnp.zeros_like(ret_s_ref)
        hg_s_ref[...] = jnp.zeros_like(hg_s_ref)
        m_s_ref[...] = jnp.zeros_like(m_s_ref)
        ext_x_ref[0:SUBLANES, :] = jnp.zeros((SUBLANES, M_DIM), F32)
        ext_b_ref[0:SUBLANES, :] = jnp.zeros((SUBLANES, M_BC), F32)

    stages = [_hgrn2_chunk(layer, hg_ref, lbl_ref, hnw_ref, tri_ref, o_ref, hg_s_ref),
              _mamba2_chunk(zx_ref, bc_ref, dt_ref, cw_ref, cb_ref, dtb_ref, alog_ref, dskip_ref, mnw_ref, tri_ref,
                            expand_ref, o_ref, m_s_ref, ext_x_ref, ext_b_ref),
              _retention_chunk(cos_ref, sin_ref, rtab_ref, ret_ref, o_ref, ret_s_ref)]
    while stages:
        for stage in list(stages):
            if next(stage, "done") == "done":
                stages.remove(stage)


def _mixers(proj, cos2, sin2, ret_tabs, lb_logits, hg_norm_w3, conv_w, conv_b3, dtb3, alog3, dskip3, m_norm_w3, tri,
            expand, w_out, w1, w2, w_in_t, layer):
    bsz, seq, _ = proj.shape
    c = CHUNK
    n_chunks = seq // c
    n_steps = bsz * n_chunks
    has_next = layer + 1 < DEPTH

    def cols(first, width):
        assert first % width == 0
        return pl.BlockSpec((1, c, width), lambda b, i: (b, i, first // width))

    def lay(shape):
        return pl.BlockSpec((1,) + shape, lambda b, i: (layer, 0, 0))

    def const(shape):
        return pl.BlockSpec(shape, lambda b, i: (0,) * len(shape))

    def slab_in(w):
        rows = w.shape[1] // n_steps
        assert rows * n_steps == w.shape[1] and rows % (2 * SUBLANES) == 0
        return pl.BlockSpec((None, rows, w.shape[2]), lambda b, i: (layer, b * n_chunks + i, 0))

    def slab_out(w):
        rows = w.shape[1] // n_steps
        return (pl.BlockSpec((rows, w.shape[2]), lambda b, i: (b * n_chunks + i, 0)),
                jax.ShapeDtypeStruct(w.shape[1:], BF16))

    tab = pl.BlockSpec((1, c, HEAD_DIM), lambda b, i: (b, i, 0))
    in_specs = [
        tab, tab, const((RET_HEADS, 3, c, c)),
        cols(*COL_RET), cols(*COL_HG), cols(*COL_ZX), cols(*COL_BC), cols(*COL_DT),
        const((DEPTH, HG_DIM)), lay((1, HG_DIM)),
        lay((M_CONV, M_CONV_DIM)), lay((1, M_CONV_DIM)), lay((1, LANES)), lay((1, LANES)), lay((1, M_DIM)),
        lay((1, M_DIM)), const((c, c)), const((LANES, M_DIM)),
        slab_in(w_out), slab_in(w1), slab_in(w2),
    ]
    args = [cos2, sin2, ret_tabs, proj, proj, proj, proj, proj, lb_logits, hg_norm_w3, conv_w, conv_b3, dtb3, alog3,
            dskip3, m_norm_w3, tri, expand, w_out, w1, w2]
    out_specs = [pl.BlockSpec((1, c, D_MODEL), lambda b, i: (b, i, 0))]
    out_shape = [jax.ShapeDtypeStruct((bsz, seq, D_MODEL), BF16)]
    for w in (w_out, w1, w2):
        spec, shape = slab_out(w)
        out_specs.append(spec)
        out_shape.append(shape)
    if has_next:
        win_in, win_out, win_shape = _win_specs(layer + 1, n_steps, lambda b, i: b * n_chunks + i)
        in_specs.append(win_in)
        args.append(w_in_t)
        out_specs.append(win_out)
        out_shape.append(win_shape)
    return pl.pallas_call(
        functools.partial(_mixers_kernel, layer, has_next, n_chunks),
        grid=(bsz, n_chunks),
        in_specs=in_specs,
        out_specs=out_specs,
        out_shape=out_shape,
        scratch_shapes=[
            pltpu.VMEM((RET_HEADS, HEAD_DIM, HEAD_DIM), F32),
            pltpu.VMEM((HG_HEADS, HEAD_DIM, HEAD_DIM), F32),
            pltpu.VMEM((M_GROUPS, M_STATE, M_DIM // M_GROUPS), F32),
            pltpu.VMEM((SUBLANES + c, M_DIM), F32),
            pltpu.VMEM((SUBLANES + c, M_BC), F32),
        ],
        compiler_params=_params("arbitrary", "arbitrary"),
        name="mixers",
    )(*args)


def _outproj_kernel(x_ref, mix_ref, w_ref, ada_ref, g_ref, b_ref, o_ref):
    gate = ada_ref[0, 2:3, :]
    for r in range(0, o_ref.shape[1], OUTPROJ_TR):
        rs = slice(r, r + OUTPROJ_TR)
        y = _dot(mix_ref[0, rs, :], w_ref[...])
        o_ref[0, rs, :] = _layer_norm(DEEPNORM_ALPHA * x_ref[0, rs, :] + (1.0 + gate) * y,
                                      g_ref[0, 0:1, :], b_ref[0, 0:1, :])


def _outproj(x, mix, w_out_b, ada, ln_g, ln_b, layer):
    bsz, seq, _ = x.shape
    tm = min(OUTPROJ_TM, seq)
    rows = pl.BlockSpec((1, tm, D_MODEL), lambda b, m: (b, m, 0))
    return pl.pallas_call(
        _outproj_kernel,
        grid=(bsz, seq // tm),
        in_specs=[
            rows, rows,
            pl.BlockSpec((D_MODEL, D_MODEL), lambda b, m: (0, 0)),
            pl.BlockSpec((1, N_ADA, D_MODEL), lambda b, m: (layer * bsz + b, 0, 0)),
            pl.BlockSpec((1, 2, D_MODEL), lambda b, m: (layer, 0, 0)),
            pl.BlockSpec((1, 2, D_MODEL), lambda b, m: (layer, 0, 0)),
        ],
        out_specs=rows,
        out_shape=jax.ShapeDtypeStruct(x.shape, F32),
        compiler_params=_params("arbitrary", "arbitrary"),
        name="outproj",
    )(x, mix, w_out_b, ada, ln_g, ln_b)


def _mlp_kernel(x_ref, ada_ref, w1_ref, w2_ref, g_ref, b_ref, o_ref, h_ref):
    f = pl.program_id(2)
    last_f = pl.num_programs(2) - 1
    blocks = [slice(r, r + MLP_TR) for r in range(0, o_ref.shape[1], MLP_TR)]

    def step(first, last):
        u2 = []
        for rs in blocks:
            if first:
                h_ref[rs, :] = (x_ref[0, rs, :] * (1.0 + ada_ref[0, 4:5, :]) + ada_ref[0, 3:4, :]).astype(BF16)
            u = jnp.maximum(_dot(h_ref[rs, :], w1_ref[...]), 0.0)
            u2.append((u * u).astype(BF16))
        for rs, u2_r in zip(blocks, u2):
            for n in range(0, D_MODEL, MLP_TN):
                y = _dot(u2_r, w2_ref[:, n:n + MLP_TN])
                if first:
                    o_ref[0, rs, n:n + MLP_TN] = y
                else:
                    o_ref[0, rs, n:n + MLP_TN] += y
            if last:
                for r in range(rs.start, rs.stop, LN_ROWS):
                    ls = slice(r, r + LN_ROWS)
                    o_ref[0, ls, :] = _layer_norm(
                        DEEPNORM_ALPHA * x_ref[0, ls, :] + (1.0 + ada_ref[0, 5:6, :]) * o_ref[0, ls, :],
                        g_ref[0, 1:2, :], b_ref[0, 1:2, :])

    pl.when(f == 0)(functools.partial(step, True, False))
    pl.when((f > 0) & (f < last_f))(functools.partial(step, False, False))
    pl.when(f == last_f)(functools.partial(step, False, True))


def _mlp(x, ada, w1_b, w2_b, ln_g, ln_b, layer):
    bsz, seq, _ = x.shape
    tm = min(MLP_TM, seq)
    tf = MLP_TF
    return pl.pallas_call(
        _mlp_kernel,
        grid=(bsz, seq // tm, D_FF // tf),
        in_specs=[
            pl.BlockSpec((1, tm, D_MODEL), lambda b, m, f: (b, m, 0)),
            pl.BlockSpec((1, N_ADA, D_MODEL), lambda b, m, f: (layer * bsz + b, 0, 0)),
            pl.BlockSpec((D_MODEL, tf), lambda b, m, f: (0, f)),
            pl.BlockSpec((tf, D_MODEL), lambda b, m, f: (f, 0)),
            pl.BlockSpec((1, 2, D_MODEL), lambda b, m, f: (layer, 0, 0)),
            pl.BlockSpec((1, 2, D_MODEL), lambda b, m, f: (layer, 0, 0)),
        ],
        out_specs=pl.BlockSpec((1, tm, D_MODEL), lambda b, m, f: (b, m, 0)),
        out_shape=jax.ShapeDtypeStruct(x.shape, F32),
        scratch_shapes=[pltpu.VMEM((tm, D_MODEL), BF16)],
        compiler_params=_params("arbitrary", "arbitrary", "arbitrary"),
        name="mlp",
    )(x, ada, w1_b, w2_b, ln_g, ln_b)


def kernel(x, c, positions, lb_logits, w_in, w_out, w_ada, b_ada, ln_g, ln_b, hg_norm_w, m_conv_w, m_conv_b,
           m_dt_bias, m_a_log, m_d, m_norm_w, w1, w2):
    bsz, seq, _ = x.shape
    assert seq % CHUNK == 0

    pad_heads = ((0, 0), (0, LANES - M_HEADS))
    dtb3 = jnp.pad(m_dt_bias, pad_heads).reshape(DEPTH, 1, LANES)
    alog3 = jnp.pad(m_a_log, pad_heads).reshape(DEPTH, 1, LANES)
    dskip3 = jnp.repeat(m_d, M_HEAD_DIM, axis=1).reshape(DEPTH, 1, M_DIM)
    conv_b3 = m_conv_b.reshape(DEPTH, 1, M_CONV_DIM)
    m_norm_w3 = m_norm_w.reshape(DEPTH, 1, M_DIM)
    hg_norm_w3 = hg_norm_w.reshape(DEPTH, 1, HG_DIM)
    pos3 = positions.reshape(bsz, seq, 1)

    inv_freq = 1.0 / (ROPE_BASE ** jnp.linspace(0.0, 1.0, HEAD_DIM // 2, dtype=F32))
    invf2 = jnp.concatenate([inv_freq, inv_freq]).reshape(1, HEAD_DIM)
    idx = jnp.arange(CHUNK)
    tri = (idx[None, :] <= idx[:, None]).astype(BF16)
    expand = (jnp.arange(M_DIM)[None, :] // M_HEAD_DIM == jnp.arange(LANES)[:, None]).astype(BF16)
    ret_tabs = _retention_tables()

    w_in_t = jnp.swapaxes(w_in, 1, 2)
    ada = _ada(c, w_ada, b_ada).reshape(DEPTH * bsz, N_ADA, D_MODEL)
    cos2, sin2, w_in_b = _rope(pos3, invf2, w_in_t)
    for l in range(DEPTH):
        proj = _inproj(x, ada, w_in_b, l)
        mix, w_out_b, w1_b, w2_b, *rest = _mixers(proj, cos2, sin2, ret_tabs, lb_logits, hg_norm_w3, m_conv_w,
                                                  conv_b3, dtb3, alog3, dskip3, m_norm_w3, tri, expand, w_out, w1,
                                                  w2, w_in_t, l)
        if rest:
            w_in_b = rest[0]
        x = _outproj(x, mix, w_out_b, ada, ln_g, ln_b, l)
        x = _mlp(x, ada, w1_b, w2_b, ln_g, ln_b, l)
    return x
```

```python
import functools
import math

import jax
import jax.numpy as jnp
from jax import lax
from jax.experimental import pallas as pl
from jax.experimental.pallas import tpu as pltpu

F32 = jnp.float32
BF16 = jnp.bfloat16

D_MODEL = 2048
DEPTH = 4
RET_HEADS = 4
RET_DIM = 512
HEAD_DIM = 128
ROPE_BASE = 10000.0
HG_HEADS = 4
HG_DIM = 512
M_DIM = 1024
M_HEAD_DIM = 64
M_HEADS = 16
M_GROUPS = 2
M_STATE = 128
M_CONV = 4
M_CONV_DIM = 1536
M_BC = M_CONV_DIM - M_DIM
D_FF = 4 * D_MODEL
IN_COLS = 6672
DEEPNORM_ALPHA = (2 * DEPTH) ** 0.25
LN_EPS = 1e-5
RMS_EPS = 1e-6
N_ADA = 6
LOG2E = math.log2(math.e)

LANES = 128
SUBLANES = 8
IN_COLS_PAD = 6912
VMEM_LIMIT = 56 * 2 ** 20

COL_RET = (0, 4 * RET_DIM)
COL_HG = (4 * RET_DIM, 4 * HG_DIM)
COL_ZX = (4 * RET_DIM + 4 * HG_DIM, 2 * M_DIM)
COL_BC = (COL_ZX[0] + 2 * M_DIM, M_BC)
COL_DT = (COL_BC[0] + M_BC, LANES)

ADA_TN = 1024
ROPE_TM = 512
INPROJ_TM, INPROJ_TN = 256, 768
OUTPROJ_TM = 512
OUTPROJ_TR = 256
MLP_TM, MLP_TF = 1024, 512
MLP_TN = 512
MLP_TR = 256
LN_ROWS = 128

CHUNK = 128
HG_LEVELS = tuple(1 << i for i in range(int(math.log2(CHUNK))))


def _params(*sem):
    return pltpu.CompilerParams(dimension_semantics=sem, vmem_limit_bytes=VMEM_LIMIT)


def _silu(x):
    h = 0.5 * x
    return h + h * jnp.tanh(h)


def _dot(a, b):
    return jnp.dot(a, b, preferred_element_type=F32)


def _dot_nt(a, b):
    return lax.dot_general(a, b, (((1,), (1,)), ((), ())), preferred_element_type=F32)


def _dot_tn(a, b):
    return lax.dot_general(a, b, (((0,), (0,)), ((), ())), preferred_element_type=F32)


def _split(x, pieces):
    out = []
    for _ in range(pieces - 1):
        p = x.astype(BF16)
        out.append(p)
        x = x - p.astype(F32)
    out.append(x.astype(BF16))
    return out


def _dot_split_rhs(m_bf16, x, pieces):
    return functools.reduce(jnp.add, [_dot(m_bf16, p) for p in _split(x, pieces)])


def _dot_split_lhs(x, m_bf16, pieces):
    return functools.reduce(jnp.add, [_dot(p, m_bf16) for p in _split(x, pieces)])


def _layer_norm(v, g, b):
    mu = jnp.mean(v, axis=-1, keepdims=True)
    d = v - mu
    var = jnp.mean(d * d, axis=-1, keepdims=True)
    return d * lax.rsqrt(var + LN_EPS) * g + b


def _win_slab_rows(n_steps):
    return LANES * pl.cdiv(IN_COLS_PAD, LANES * n_steps)


def _win_specs(layer, n_steps, step_of):
    rows = _win_slab_rows(n_steps)
    last_in = pl.cdiv(IN_COLS, rows) - 1
    last_out = pl.cdiv(IN_COLS_PAD, rows) - 1
    assert last_out < n_steps
    return (pl.BlockSpec((None, rows, D_MODEL), lambda *g: (layer, jnp.minimum(step_of(*g), last_in), 0)),
            pl.BlockSpec((rows, D_MODEL), lambda *g: (jnp.minimum(step_of(*g), last_out), 0)),
            jax.ShapeDtypeStruct(((last_out + 1) * rows, D_MODEL), BF16))


def _win_convert(win_ref, win_o_ref, step):
    rows = win_o_ref.shape[0]
    blk = jnp.minimum(step, pl.cdiv(IN_COLS_PAD, rows) - 1)
    n_idx = blk * rows + lax.broadcasted_iota(jnp.int32, (rows, 1), 0)
    win_o_ref[...] = jnp.where(n_idx < IN_COLS, win_ref[...], 0.0).astype(BF16)


def _ada_kernel(c_ref, w_ref, b_ref, o_ref):
    cond = _silu(c_ref[...]).astype(BF16)
    o_ref[0] = _dot(cond, w_ref[0].astype(BF16)) + b_ref[0]


def _ada(c, w_ada, b_ada):
    bsz = c.shape[0]
    tn = ADA_TN
    return pl.pallas_call(
        _ada_kernel,
        grid=(DEPTH, N_ADA * D_MODEL // tn),
        in_specs=[
            pl.BlockSpec((bsz, D_MODEL), lambda l, n: (0, 0)),
            pl.BlockSpec((1, D_MODEL, tn), lambda l, n: (l, 0, n)),
            pl.BlockSpec((1, 1, tn), lambda l, n: (l, 0, n)),
        ],
        out_specs=pl.BlockSpec((1, bsz, tn), lambda l, n: (l, 0, n)),
        out_shape=jax.ShapeDtypeStruct((DEPTH, bsz, N_ADA * D_MODEL), F32),
        compiler_params=_params("arbitrary", "arbitrary"),
        name="ada",
    )(c, w_ada, b_ada.reshape(DEPTH, 1, N_ADA * D_MODEL))


def _rope_kernel(n_tiles, pos_ref, invf_ref, win_ref, cos_ref, sin_ref, win_o_ref):
    _win_convert(win_ref, win_o_ref, pl.program_id(0) * n_tiles + pl.program_id(1))
    ang = pos_ref[0].astype(F32) * invf_ref[...]
    lane = lax.broadcasted_iota(jnp.int32, (1, HEAD_DIM), 1)
    cos_ref[0] = jnp.cos(ang)
    sin_ref[0] = jnp.where(lane < HEAD_DIM // 2, -1.0, 1.0) * jnp.sin(ang)


def _rope(pos3, invf2, w_in_t):
    bsz, seq, _ = pos3.shape
    tm = min(ROPE_TM, seq)
    n_tiles = seq // tm
    tab = pl.BlockSpec((1, tm, HEAD_DIM), lambda b, m: (b, m, 0))
    shape = jax.ShapeDtypeStruct((bsz, seq, HEAD_DIM), F32)
    win_in, win_out, win_shape = _win_specs(0, bsz * n_tiles, lambda b, m: b * n_tiles + m)
    return pl.pallas_call(
        functools.partial(_rope_kernel, n_tiles),
        grid=(bsz, n_tiles),
        in_specs=[pl.BlockSpec((1, tm, 1), lambda b, m: (b, m, 0)),
                  pl.BlockSpec((1, HEAD_DIM), lambda b, m: (0, 0)),
                  win_in],
        out_specs=(tab, tab, win_out),
        out_shape=(shape, shape, win_shape),
        compiler_params=_params("arbitrary", "arbitrary"),
        name="rope",
    )(pos3, invf2, w_in_t)


def _inproj_kernel(x_ref, ada_ref, wt_ref, o_ref):
    sh = ada_ref[0, 0:1, :]
    sc = ada_ref[0, 1:2, :]
    h = (x_ref[0] * (1.0 + sc) + sh).astype(BF16)
    for n in range(0, IN_COLS_PAD, INPROJ_TN):
        o_ref[0, :, n:n + INPROJ_TN] = _dot_nt(h, wt_ref[n:n + INPROJ_TN, :])


def _inproj(x, ada, w_in_t, layer):
    bsz, seq, _ = x.shape
    tm = min(INPROJ_TM, seq)
    return pl.pallas_call(
        _inproj_kernel,
        grid=(bsz, seq // tm),
        in_specs=[
            pl.BlockSpec((1, tm, D_MODEL), lambda b, m: (b, m, 0)),
            pl.BlockSpec((1, N_ADA, D_MODEL), lambda b, m: (layer * bsz + b, 0, 0)),
            pl.BlockSpec((IN_COLS_PAD, D_MODEL), lambda b, m: (0, 0), pipeline_mode=pl.Buffered(1)),
        ],
        out_specs=pl.BlockSpec((1, tm, IN_COLS_PAD), lambda b, m: (b, m, 0)),
        out_shape=jax.ShapeDtypeStruct((bsz, seq, IN_COLS_PAD), F32),
        compiler_params=_params("arbitrary", "arbitrary"),
        name="inproj",
    )(x, ada, w_in_t)


def _retention_tables():
    c = CHUNK
    i = jnp.arange(c, dtype=F32)[:, None]
    j = jnp.arange(c, dtype=F32)[None, :]
    ones = jnp.ones((1, HEAD_DIM), F32)
    tabs = []
    for h in range(RET_HEADS):
        log_gamma = math.log(1.0 - 2.0 ** (-5.0 - h))
        decay = jnp.where(i >= j, jnp.exp((i - j) * log_gamma), 0.0)
        tabs.append(jnp.stack([decay, jnp.exp((i + 1.0) * log_gamma) * ones, jnp.exp((c - 1.0 - i) * log_gamma) * ones]))
    return jnp.stack(tabs)


def _retention_chunk(cos_ref, sin_ref, tab_ref, p_ref, o_ref, s_ref):
    c = CHUNK
    cos2 = cos_ref[0]
    sin2 = sin_ref[0]

    def rot(t):
        return t * cos2 + pltpu.roll(t, HEAD_DIM // 2, 1) * sin2

    for h in range(RET_HEADS):
        log_gamma = math.log(1.0 - 2.0 ** (-5.0 - h))
        q = rot(p_ref[0, :, h * HEAD_DIM:(h + 1) * HEAD_DIM])
        k = rot(p_ref[0, :, RET_DIM + h * HEAD_DIM:RET_DIM + (h + 1) * HEAD_DIM]) * (HEAD_DIM ** -0.5)
        v = p_ref[0, :, 2 * RET_DIM + h * HEAD_DIM:2 * RET_DIM + (h + 1) * HEAD_DIM].astype(BF16)
        g = p_ref[0, :, 3 * RET_DIM + h * HEAD_DIM:3 * RET_DIM + (h + 1) * HEAD_DIM]
        scores = _dot_nt(q.astype(BF16), k.astype(BF16)) * tab_ref[h, 0]
        st = s_ref[h]
        y = _dot(scores.astype(BF16), v) + _dot((q * tab_ref[h, 1]).astype(BF16), st.astype(BF16))
        s_ref[h] = math.exp(c * log_gamma) * st + _dot_tn((k * tab_ref[h, 2]).astype(BF16), v)
        o = y * lax.rsqrt(jnp.mean(y * y, axis=-1, keepdims=True) + RMS_EPS)
        o_ref[0, :, h * HEAD_DIM:(h + 1) * HEAD_DIM] = (_silu(g) * o).astype(BF16)
        yield


def _hgrn2_chunk(layer, p_ref, lbl_ref, nw_ref, tri_ref, o_ref, s_ref):
    c = CHUNK
    lbl = lbl_ref[...]
    e = jnp.exp(lbl - jnp.max(lbl, axis=0, keepdims=True))
    p = e / jnp.sum(e, axis=0, keepdims=True)
    lb = jnp.zeros((1, HG_DIM), F32)
    for d in range(1, layer + 1):
        lb = lb + p[d:d + 1, :]

    q = p_ref[0, :, 0:HG_DIM]
    f = lb + (1.0 - lb) * jax.nn.sigmoid(p_ref[0, :, HG_DIM:2 * HG_DIM])
    log_f = jnp.log(f)
    k = 1.0 - f
    v_bf = p_ref[0, :, 2 * HG_DIM:3 * HG_DIM].astype(BF16)
    q_bf = q.astype(BF16)
    k_bf = k.astype(BF16)

    cum = _dot_split_rhs(tri_ref[...], log_f, 3)
    row = lax.broadcasted_iota(jnp.int32, (c, 1), 0)
    col = lax.broadcasted_iota(jnp.int32, (1, c), 1)
    r_sub = row & (SUBLANES - 1)
    yield

    def pick_row(x, block, j):
        x3 = x.reshape(c // block, block, HG_DIM)[:, j:j + 1, :]
        return jnp.broadcast_to(x3, (c // block, block, HG_DIM)).reshape(c, HG_DIM)

    merged, masks = [], []
    for b in HG_LEVELS:
        sh = int(math.log2(b))
        upper = ((row >> sh) & 1) == 1
        if b == 1:
            m = jnp.where(upper, q * f, k)
        else:
            if 2 * b >= SUBLANES:
                ref = pick_row(cum, 2 * b, b - 1)
            else:
                ref = jnp.where(r_sub < 2 * b, pick_row(cum, SUBLANES, b - 1), pick_row(cum, SUBLANES, 3 * b - 1))
            m = jnp.where(upper, q, k) * jnp.exp2((cum - ref) * jnp.where(upper, LOG2E, -LOG2E))
        merged.append(m.astype(BF16))
        masks.append(((row >> (sh + 1)) == (col >> (sh + 1))) & upper & (((col >> sh) & 1) == 0))
        yield

    tail = cum[c - 1:c, :] - cum
    q_in = (q * jnp.exp(cum)).astype(BF16)
    k_st = (k * jnp.exp(tail)).astype(BF16)
    e_tot = jnp.exp(cum[c - 1:c, :])

    outs = []
    for h in range(HG_HEADS):
        sl = slice(h * HEAD_DIM, (h + 1) * HEAD_DIM)
        scores = jnp.where(row == col, _dot_nt(q_bf[:, sl], k_bf[:, sl]), 0.0)
        for lv in range(len(HG_LEVELS)):
            m_h = merged[lv][:, sl]
            scores = jnp.where(masks[lv], _dot_nt(m_h, m_h), scores)
        st_t = s_ref[h]
        y = _dot(scores.astype(BF16), v_bf[:, sl]) + _dot_nt(q_in[:, sl], st_t.astype(BF16))
        s_ref[h] = st_t * e_tot[:, sl] + _dot_tn(v_bf[:, sl], k_st[:, sl])
        outs.append(y * lax.rsqrt(jnp.mean(y * y, axis=-1, keepdims=True) + RMS_EPS))
        yield
    o = jnp.concatenate(outs, axis=1) * nw_ref[0] * _silu(p_ref[0, :, 3 * HG_DIM:4 * HG_DIM])
    o_ref[0, :, RET_DIM:RET_DIM + HG_DIM] = o.astype(BF16)


def _mamba2_chunk(zx_ref, bc_ref, dt_ref, cw_ref, cb_ref, dtb_ref, alog_ref, dskip_ref, nw_ref, tri_ref, expand_ref,
                  o_ref, s_ref, ext_x_ref, ext_b_ref):
    c = CHUNK
    pad = SUBLANES

    r_sub = lax.broadcasted_iota(jnp.int32, (pad, 1), 0)

    def conv_silu(tail_ref, u, lo, hi):
        prev = tail_ref[...]
        acc = cb_ref[0, :, lo:hi] + cw_ref[0, M_CONV - 1:M_CONV, lo:hi] * u
        for s in range(1, M_CONV):
            rolled = pltpu.roll(u, s, 0)
            first = jnp.where(r_sub < s, pltpu.roll(prev, s, 0), rolled[0:pad])
            acc = acc + cw_ref[0, M_CONV - 1 - s:M_CONV - s, lo:hi] * jnp.concatenate([first, rolled[pad:]], axis=0)
        tail_ref[...] = u[c - pad:c]
        return _silu(acc)

    xs = conv_silu(ext_x_ref, zx_ref[0, :, M_DIM:], 0, M_DIM)
    yield
    bc = conv_silu(ext_b_ref, bc_ref[0], M_DIM, M_CONV_DIM)
    yield

    dt = jax.nn.softplus(dt_ref[0] + dtb_ref[0])
    a = -jnp.exp(alog_ref[0])
    cum = _dot_split_rhs(tri_ref[...], dt * a, 3)
    cum_t = cum.T
    e_cum = jnp.exp(cum)
    w_state = dt * jnp.exp(cum[c - 1:c, :] - cum)
    spread = _dot_split_lhs(jnp.concatenate([dt, e_cum, w_state], axis=0), expand_ref[...], 2)
    dt_e, e_cum_e, w_state_e = spread[0:c], spread[c:2 * c], spread[2 * c:3 * c]
    x_dt = (xs * dt_e).astype(BF16)
    x_st = (xs * w_state_e).astype(BF16)
    yield

    row = lax.broadcasted_iota(jnp.int32, (c, 1), 0)
    col = lax.broadcasted_iota(jnp.int32, (1, c), 1)
    causal = row >= col
    lane = lax.broadcasted_iota(jnp.int32, (1, LANES), 1)
    gw = M_DIM // M_GROUPS
    heads_per_group = M_HEADS // M_GROUPS

    ys = []
    for g in range(M_GROUPS):
        b_g = bc[:, g * M_STATE:(g + 1) * M_STATE].astype(BF16)
        c_g = bc[:, (M_GROUPS + g) * M_STATE:(M_GROUPS + g + 1) * M_STATE].astype(BF16)
        gram = _dot_nt(c_g, b_g)
        st = s_ref[g]
        gsl = slice(g * gw, (g + 1) * gw)
        y_g = _dot(c_g, st.astype(BF16)) * e_cum_e[:, gsl]
        s_ref[g] = st * e_cum_e[c - 1:c, gsl] + _dot_tn(b_g, x_st[:, gsl])
        pieces = []
        for pr in range(heads_per_group // 2):
            psl = slice(g * gw + pr * LANES, g * gw + (pr + 1) * LANES)
            x_pair = x_dt[:, psl]
            acc = jnp.zeros((c, LANES), F32)
            for half in range(2):
                h = g * heads_per_group + 2 * pr + half
                seg = cum[:, h:h + 1] - cum_t[h:h + 1, :]
                decay = jnp.exp(jnp.where(causal, seg, -jnp.inf))
                in_head = (lane >= half * M_HEAD_DIM) & (lane < (half + 1) * M_HEAD_DIM)
                acc = acc + _dot((gram * decay).astype(BF16), jnp.where(in_head, x_pair, jnp.zeros_like(x_pair)))
            pieces.append(acc)
            yield
        ys.append(y_g + jnp.concatenate(pieces, axis=1))
    y = jnp.concatenate(ys, axis=1) + xs * dskip_ref[0]
    y = y * _silu(zx_ref[0, :, 0:M_DIM])
    outs = []
    for g in range(M_GROUPS):
        yg = y[:, g * gw:(g + 1) * gw]
        outs.append(yg * lax.rsqrt(jnp.mean(yg * yg, axis=-1, keepdims=True) + RMS_EPS))
    o_ref[0, :, RET_DIM + HG_DIM:] = (jnp.concatenate(outs, axis=1) * nw_ref[0]).astype(BF16)


def _mixers_kernel(layer, has_next, n_chunks, *refs):
    (cos_ref, sin_ref, rtab_ref, ret_ref, hg_ref, zx_ref, bc_ref, dt_ref, lbl_ref, hnw_ref, cw_ref, cb_ref, dtb_ref,
     alog_ref, dskip_ref, mnw_ref, tri_ref, expand_ref, wout_ref, w1_ref, w2_ref) = refs[:21]
    refs = refs[21:]
    if has_next:
        win_ref, refs = refs[0], refs[1:]
    o_ref, wout_o_ref, w1_o_ref, w2_o_ref = refs[:4]
    refs = refs[4:]
    if has_next:
        win_o_ref, refs = refs[0], refs[1:]
    ret_s_ref, hg_s_ref, m_s_ref, ext_x_ref, ext_b_ref = refs

    wout_o_ref[...] = wout_ref[...].astype(BF16)
    w1_o_ref[...] = w1_ref[...].astype(BF16)
    w2_o_ref[...] = w2_ref[...].astype(BF16)
    if has_next:
        _win_convert(win_ref, win_o_ref, pl.program_id(0) * n_chunks + pl.program_id(1))

    @pl.when(pl.program_id(1) == 0)
    def _():
        ret_s_ref[...] = jnp.zeros_like(ret_s_ref)
        hg_s_ref[...] = jnp.zeros_like(hg_s_ref)
        m_s_ref[...] = jnp.zeros_like(m_s_ref)
        ext_x_ref[...] = jnp.zeros_like(ext_x_ref)
        ext_b_ref[...] = jnp.zeros_like(ext_b_ref)

    stages = [_hgrn2_chunk(layer, hg_ref, lbl_ref, hnw_ref, tri_ref, o_ref, hg_s_ref),
              _mamba2_chunk(zx_ref, bc_ref, dt_ref, cw_ref, cb_ref, dtb_ref, alog_ref, dskip_ref, mnw_ref, tri_ref,
                            expand_ref, o_ref, m_s_ref, ext_x_ref, ext_b_ref),
              _retention_chunk(cos_ref, sin_ref, rtab_ref, ret_ref, o_ref, ret_s_ref)]
    while stages:
        for stage in list(stages):
            if next(stage, "done") == "done":
                stages.remove(stage)


def _mixers(proj, cos2, sin2, ret_tabs, lb_logits, hg_norm_w3, conv_w, conv_b3, dtb3, alog3, dskip3, m_norm_w3, tri,
            expand, w_out, w1, w2, w_in_t, layer):
    bsz, seq, _ = proj.shape
    c = CHUNK
    n_chunks = seq // c
    n_steps = bsz * n_chunks
    has_next = layer + 1 < DEPTH

    def cols(first, width):
        assert first % width == 0
        return pl.BlockSpec((1, c, width), lambda b, i: (b, i, first // width))

    def lay(shape):
        return pl.BlockSpec((1,) + shape, lambda b, i: (layer, 0, 0))

    def const(shape):
        return pl.BlockSpec(shape, lambda b, i: (0,) * len(shape))

    def slab_in(w):
        rows = w.shape[1] // n_steps
        assert rows * n_steps == w.shape[1] and rows % (2 * SUBLANES) == 0
        return pl.BlockSpec((None, rows, w.shape[2]), lambda b, i: (layer, b * n_chunks + i, 0))

    def slab_out(w):
        rows = w.shape[1] // n_steps
        return (pl.BlockSpec((rows, w.shape[2]), lambda b, i: (b * n_chunks + i, 0)),
                jax.ShapeDtypeStruct(w.shape[1:], BF16))

    tab = pl.BlockSpec((1, c, HEAD_DIM), lambda b, i: (b, i, 0))
    in_specs = [
        tab, tab, const((RET_HEADS, 3, c, c)),
        cols(*COL_RET), cols(*COL_HG), cols(*COL_ZX), cols(*COL_BC), cols(*COL_DT),
        const((DEPTH, HG_DIM)), lay((1, HG_DIM)),
        lay((M_CONV, M_CONV_DIM)), lay((1, M_CONV_DIM)), lay((1, LANES)), lay((1, LANES)), lay((1, M_DIM)),
        lay((1, M_DIM)), const((c, c)), const((LANES, M_DIM)),
        slab_in(w_out), slab_in(w1), slab_in(w2),
    ]
    args = [cos2, sin2, ret_tabs, proj, proj, proj, proj, proj, lb_logits, hg_norm_w3, conv_w, conv_b3, dtb3, alog3,
            dskip3, m_norm_w3, tri, expand, w_out, w1, w2]
    out_specs = [pl.BlockSpec((1, c, D_MODEL), lambda b, i: (b, i, 0))]
    out_shape = [jax.ShapeDtypeStruct((bsz, seq, D_MODEL), BF16)]
    for w in (w_out, w1, w2):
        spec, shape = slab_out(w)
        out_specs.append(spec)
        out_shape.append(shape)
    if has_next:
        win_in, win_out, win_shape = _win_specs(layer + 1, n_steps, lambda b, i: b * n_chunks + i)
        in_specs.append(win_in)
        args.append(w_in_t)
        out_specs.append(win_out)
        out_shape.append(win_shape)
    return pl.pallas_call(
        functools.partial(_mixers_kernel, layer, has_next, n_chunks),
        grid=(bsz, n_chunks),
        in_specs=in_specs,
        out_specs=out_specs,
        out_shape=out_shape,
        scratch_shapes=[
            pltpu.VMEM((RET_HEADS, HEAD_DIM, HEAD_DIM), F32),
            pltpu.VMEM((HG_HEADS, HEAD_DIM, HEAD_DIM), F32),
            pltpu.VMEM((M_GROUPS, M_STATE, M_DIM // M_GROUPS), F32),
            pltpu.VMEM((SUBLANES, M_DIM), F32),
            pltpu.VMEM((SUBLANES, M_BC), F32),
        ],
        compiler_params=_params("arbitrary", "arbitrary"),
        name="mixers",
    )(*args)


def _outproj_kernel(x_ref, mix_ref, w_ref, ada_ref, g_ref, b_ref, o_ref):
    gate = ada_ref[0, 2:3, :]
    for r in range(0, o_ref.shape[1], OUTPROJ_TR):
        rs = slice(r, r + OUTPROJ_TR)
        y = _dot(mix_ref[0, rs, :], w_ref[...])
        o_ref[0, rs, :] = _layer_norm(DEEPNORM_ALPHA * x_ref[0, rs, :] + (1.0 + gate) * y,
                                      g_ref[0, 0:1, :], b_ref[0, 0:1, :])


def _outproj(x, mix, w_out_b, ada, ln_g, ln_b, layer):
    bsz, seq, _ = x.shape
    tm = min(OUTPROJ_TM, seq)
    rows = pl.BlockSpec((1, tm, D_MODEL), lambda b, m: (b, m, 0))
    return pl.pallas_call(
        _outproj_kernel,
        grid=(bsz, seq // tm),
        in_specs=[
            rows, rows,
            pl.BlockSpec((D_MODEL, D_MODEL), lambda b, m: (0, 0)),
            pl.BlockSpec((1, N_ADA, D_MODEL), lambda b, m: (layer * bsz + b, 0, 0)),
            pl.BlockSpec((1, 2, D_MODEL), lambda b, m: (layer, 0, 0)),
            pl.BlockSpec((1, 2, D_MODEL), lambda b, m: (layer, 0, 0)),
        ],
        out_specs=rows,
        out_shape=jax.ShapeDtypeStruct(x.shape, F32),
        compiler_params=_params("arbitrary", "arbitrary"),
        name="outproj",
    )(x, mix, w_out_b, ada, ln_g, ln_b)


def _mlp_kernel(x_ref, ada_ref, w1_ref, w2_ref, g_ref, b_ref, o_ref, h_ref):
    f = pl.program_id(2)
    last_f = pl.num_programs(2) - 1
    blocks = [slice(r, r + MLP_TR) for r in range(0, o_ref.shape[1], MLP_TR)]

    def step(first, last):
        u2 = []
        for rs in blocks:
            if first:
                h_ref[rs, :] = (x_ref[0, rs, :] * (1.0 + ada_ref[0, 4:5, :]) + ada_ref[0, 3:4, :]).astype(BF16)
            u = jnp.maximum(_dot(h_ref[rs, :], w1_ref[...]), 0.0)
            u2.append((u * u).astype(BF16))
        for rs, u2_r in zip(blocks, u2):
            for n in range(0, D_MODEL, MLP_TN):
                y = _dot(u2_r, w2_ref[:, n:n + MLP_TN])
                if first:
                    o_ref[0, rs, n:n + MLP_TN] = y
                else:
                    o_ref[0, rs, n:n + MLP_TN] += y
            if last:
                for r in range(rs.start, rs.stop, LN_ROWS):
                    ls = slice(r, r + LN_ROWS)
                    o_ref[0, ls, :] = _layer_norm(
                        DEEPNORM_ALPHA * x_ref[0, ls, :] + (1.0 + ada_ref[0, 5:6, :]) * o_ref[0, ls, :],
                        g_ref[0, 1:2, :], b_ref[0, 1:2, :])

    pl.when(f == 0)(functools.partial(step, True, False))
    pl.when((f > 0) & (f < last_f))(functools.partial(step, False, False))
    pl.when(f == last_f)(functools.partial(step, False, True))


def _mlp(x, ada, w1_b, w2_b, ln_g, ln_b, layer):
    bsz, seq, _ = x.shape
    tm = min(MLP_TM, seq)
    tf = MLP_TF
    return pl.pallas_call(
        _mlp_kernel,
        grid=(bsz, seq // tm, D_FF // tf),
        in_specs=[
            pl.BlockSpec((1, tm, D_MODEL), lambda b, m, f: (b, m, 0)),
            pl.BlockSpec((1, N_ADA, D_MODEL), lambda b, m, f: (layer * bsz + b, 0, 0)),
            pl.BlockSpec((D_MODEL, tf), lambda b, m, f: (0, f)),
            pl.BlockSpec((tf, D_MODEL), lambda b, m, f: (f, 0)),
            pl.BlockSpec((1, 2, D_MODEL), lambda b, m, f: (layer, 0, 0)),
            pl.BlockSpec((1, 2, D_MODEL), lambda b, m, f: (layer, 0, 0)),
        ],
        out_specs=pl.BlockSpec((1, tm, D_MODEL), lambda b, m, f: (b, m, 0)),
        out_shape=jax.ShapeDtypeStruct(x.shape, F32),
        scratch_shapes=[pltpu.VMEM((tm, D_MODEL), BF16)],
        compiler_params=_params("arbitrary", "arbitrary", "arbitrary"),
        name="mlp",
    )(x, ada, w1_b, w2_b, ln_g, ln_b)


def kernel(x, c, positions, lb_logits, w_in, w_out, w_ada, b_ada, ln_g, ln_b, hg_norm_w, m_conv_w, m_conv_b,
           m_dt_bias, m_a_log, m_d, m_norm_w, w1, w2):
    bsz, seq, _ = x.shape
    assert seq % CHUNK == 0

    pad_heads = ((0, 0), (0, LANES - M_HEADS))
    dtb3 = jnp.pad(m_dt_bias, pad_heads).reshape(DEPTH, 1, LANES)
    alog3 = jnp.pad(m_a_log, pad_heads).reshape(DEPTH, 1, LANES)
    dskip3 = jnp.repeat(m_d, M_HEAD_DIM, axis=1).reshape(DEPTH, 1, M_DIM)
    conv_b3 = m_conv_b.reshape(DEPTH, 1, M_CONV_DIM)
    m_norm_w3 = m_norm_w.reshape(DEPTH, 1, M_DIM)
    hg_norm_w3 = hg_norm_w.reshape(DEPTH, 1, HG_DIM)
    pos3 = positions.reshape(bsz, seq, 1)

    inv_freq = 1.0 / (ROPE_BASE ** jnp.linspace(0.0, 1.0, HEAD_DIM // 2, dtype=F32))
    invf2 = jnp.concatenate([inv_freq, inv_freq]).reshape(1, HEAD_DIM)
    idx = jnp.arange(CHUNK)
    tri = (idx[None, :] <= idx[:, None]).astype(BF16)
    expand = (jnp.arange(M_DIM)[None, :] // M_HEAD_DIM == jnp.arange(LANES)[:, None]).astype(BF16)
    ret_tabs = _retention_tables()

    w_in_t = jnp.swapaxes(w_in, 1, 2)
    ada = _ada(c, w_ada, b_ada).reshape(DEPTH * bsz, N_ADA, D_MODEL)
    cos2, sin2, w_in_b = _rope(pos3, invf2, w_in_t)
    for l in range(DEPTH):
        proj = _inproj(x, ada, w_in_b, l)
        mix, w_out_b, w1_b, w2_b, *rest = _mixers(proj, cos2, sin2, ret_tabs, lb_logits, hg_norm_w3, m_conv_w,
                                                  conv_b3, dtb3, alog3, dskip3, m_norm_w3, tri, expand, w_out, w1,
                                                  w2, w_in_t, l)
        if rest:
            w_in_b = rest[0]
        x = _outproj(x, mix, w_out_b, ada, ln_g, ln_b, l)
        x = _mlp(x, ada, w1_b, w2_b, ln_g, ln_b, l)
    return x
```

```python
import functools
import math

import jax
import jax.numpy as jnp
from jax import lax
from jax.experimental import pallas as pl
from jax.experimental.pallas import tpu as pltpu

F32 = jnp.float32
BF16 = jnp.bfloat16

D_MODEL = 2048
DEPTH = 4
RET_HEADS = 4
RET_DIM = 512
HEAD_DIM = 128
ROPE_BASE = 10000.0
HG_HEADS = 4
HG_DIM = 512
M_DIM = 1024
M_HEAD_DIM = 64
M_HEADS = 16
M_GROUPS = 2
M_STATE = 128
M_CONV = 4
M_CONV_DIM = 1536
M_BC = M_CONV_DIM - M_DIM
D_FF = 4 * D_MODEL
IN_COLS = 6672
DEEPNORM_ALPHA = (2 * DEPTH) ** 0.25
LN_EPS = 1e-5
RMS_EPS = 1e-6
N_ADA = 6
LOG2E = math.log2(math.e)

LANES = 128
SUBLANES = 8
IN_COLS_PAD = 6912
VMEM_LIMIT = 56 * 2 ** 20

COL_RET = (0, 4 * RET_DIM)
COL_HG = (4 * RET_DIM, 4 * HG_DIM)
COL_ZX = (4 * RET_DIM + 4 * HG_DIM, 2 * M_DIM)
COL_BC = (COL_ZX[0] + 2 * M_DIM, M_BC)
COL_DT = (COL_BC[0] + M_BC, LANES)

ADA_TN = 1024
ROPE_TM = 512
INPROJ_TM, INPROJ_TN = 256, 768
OUTPROJ_TM = 512
OUTPROJ_TR = 256
MLP_TM, MLP_TF = 1024, 512
MLP_TN = 512
MLP_TR = 256
LN_ROWS = 128

CHUNK = 128
HG_LEVELS = tuple(1 << i for i in range(int(math.log2(CHUNK))))


def _params(*sem):
    return pltpu.CompilerParams(dimension_semantics=sem, vmem_limit_bytes=VMEM_LIMIT)


def _silu(x):
    h = 0.5 * x
    return h + h * jnp.tanh(h)


def _dot(a, b):
    return jnp.dot(a, b, preferred_element_type=F32)


def _dot_nt(a, b):
    return lax.dot_general(a, b, (((1,), (1,)), ((), ())), preferred_element_type=F32)


def _dot_tn(a, b):
    return lax.dot_general(a, b, (((0,), (0,)), ((), ())), preferred_element_type=F32)


def _split(x, pieces):
    out = []
    for _ in range(pieces - 1):
        p = x.astype(BF16)
        out.append(p)
        x = x - p.astype(F32)
    out.append(x.astype(BF16))
    return out


def _dot_split_rhs(m_bf16, x, pieces):
    return functools.reduce(jnp.add, [_dot(m_bf16, p) for p in _split(x, pieces)])


def _dot_split_lhs(x, m_bf16, pieces):
    return functools.reduce(jnp.add, [_dot(p, m_bf16) for p in _split(x, pieces)])


def _layer_norm(v, g, b):
    mu = jnp.mean(v, axis=-1, keepdims=True)
    d = v - mu
    var = jnp.mean(d * d, axis=-1, keepdims=True)
    return d * lax.rsqrt(var + LN_EPS) * g + b


def _win_slab_rows(n_steps):
    return LANES * pl.cdiv(IN_COLS_PAD, LANES * n_steps)


def _win_specs(layer, n_steps, step_of):
    rows = _win_slab_rows(n_steps)
    last_in = pl.cdiv(IN_COLS, rows) - 1
    last_out = pl.cdiv(IN_COLS_PAD, rows) - 1
    assert last_out < n_steps
    return (pl.BlockSpec((None, rows, D_MODEL), lambda *g: (layer, jnp.minimum(step_of(*g), last_in), 0)),
            pl.BlockSpec((rows, D_MODEL), lambda *g: (jnp.minimum(step_of(*g), last_out), 0)),
            jax.ShapeDtypeStruct(((last_out + 1) * rows, D_MODEL), BF16))


def _win_convert(win_ref, win_o_ref, step):
    rows = win_o_ref.shape[0]
    blk = jnp.minimum(step, pl.cdiv(IN_COLS_PAD, rows) - 1)
    n_idx = blk * rows + lax.broadcasted_iota(jnp.int32, (rows, 1), 0)
    win_o_ref[...] = jnp.where(n_idx < IN_COLS, win_ref[...], 0.0).astype(BF16)


def _ada_kernel(c_ref, w_ref, b_ref, o_ref):
    cond = _silu(c_ref[...]).astype(BF16)
    o_ref[0] = _dot(cond, w_ref[0].astype(BF16)) + b_ref[0]


def _ada(c, w_ada, b_ada):
    bsz = c.shape[0]
    tn = ADA_TN
    return pl.pallas_call(
        _ada_kernel,
        grid=(DEPTH, N_ADA * D_MODEL // tn),
        in_specs=[
            pl.BlockSpec((bsz, D_MODEL), lambda l, n: (0, 0)),
            pl.BlockSpec((1, D_MODEL, tn), lambda l, n: (l, 0, n)),
            pl.BlockSpec((1, 1, tn), lambda l, n: (l, 0, n)),
        ],
        out_specs=pl.BlockSpec((1, bsz, tn), lambda l, n: (l, 0, n)),
        out_shape=jax.ShapeDtypeStruct((DEPTH, bsz, N_ADA * D_MODEL), F32),
        compiler_params=_params("arbitrary", "arbitrary"),
        name="ada",
    )(c, w_ada, b_ada.reshape(DEPTH, 1, N_ADA * D_MODEL))


def _rope_kernel(n_tiles, pos_ref, invf_ref, win_ref, cos_ref, sin_ref, win_o_ref):
    _win_convert(win_ref, win_o_ref, pl.program_id(0) * n_tiles + pl.program_id(1))
    ang = pos_ref[0].astype(F32) * invf_ref[...]
    lane = lax.broadcasted_iota(jnp.int32, (1, HEAD_DIM), 1)
    cos_ref[0] = jnp.cos(ang)
    sin_ref[0] = jnp.where(lane < HEAD_DIM // 2, -1.0, 1.0) * jnp.sin(ang)


def _rope(pos3, invf2, w_in_t):
    bsz, seq, _ = pos3.shape
    tm = min(ROPE_TM, seq)
    n_tiles = seq // tm
    tab = pl.BlockSpec((1, tm, HEAD_DIM), lambda b, m: (b, m, 0))
    shape = jax.ShapeDtypeStruct((bsz, seq, HEAD_DIM), F32)
    win_in, win_out, win_shape = _win_specs(0, bsz * n_tiles, lambda b, m: b * n_tiles + m)
    return pl.pallas_call(
        functools.partial(_rope_kernel, n_tiles),
        grid=(bsz, n_tiles),
        in_specs=[pl.BlockSpec((1, tm, 1), lambda b, m: (b, m, 0)),
                  pl.BlockSpec((1, HEAD_DIM), lambda b, m: (0, 0)),
                  win_in],
        out_specs=(tab, tab, win_out),
        out_shape=(shape, shape, win_shape),
        compiler_params=_params("arbitrary", "arbitrary"),
        name="rope",
    )(pos3, invf2, w_in_t)


def _inproj_kernel(x_ref, ada_ref, wt_ref, o_ref):
    sh = ada_ref[0, 0:1, :]
    sc = ada_ref[0, 1:2, :]
    h = (x_ref[0] * (1.0 + sc) + sh).astype(BF16)
    for n in range(0, IN_COLS_PAD, INPROJ_TN):
        o_ref[0, :, n:n + INPROJ_TN] = _dot_nt(h, wt_ref[n:n + INPROJ_TN, :])


def _inproj(x, ada, w_in_t, layer):
    bsz, seq, _ = x.shape
    tm = min(INPROJ_TM, seq)
    return pl.pallas_call(
        _inproj_kernel,
        grid=(bsz, seq // tm),
        in_specs=[
            pl.BlockSpec((1, tm, D_MODEL), lambda b, m: (b, m, 0)),
            pl.BlockSpec((1, N_ADA, D_MODEL), lambda b, m: (layer * bsz + b, 0, 0)),
            pl.BlockSpec((IN_COLS_PAD, D_MODEL), lambda b, m: (0, 0), pipeline_mode=pl.Buffered(1)),
        ],
        out_specs=pl.BlockSpec((1, tm, IN_COLS_PAD), lambda b, m: (b, m, 0)),
        out_shape=jax.ShapeDtypeStruct((bsz, seq, IN_COLS_PAD), F32),
        compiler_params=_params("arbitrary", "arbitrary"),
        name="inproj",
    )(x, ada, w_in_t)


def _retention_tables():
    c = CHUNK
    i = jnp.arange(c, dtype=F32)[:, None]
    j = jnp.arange(c, dtype=F32)[None, :]
    ones = jnp.ones((1, HEAD_DIM), F32)
    tabs = []
    for h in range(RET_HEADS):
        log_gamma = math.log(1.0 - 2.0 ** (-5.0 - h))
        decay = jnp.where(i >= j, jnp.exp((i - j) * log_gamma), 0.0)
        tabs.append(jnp.stack([decay, jnp.exp((i + 1.0) * log_gamma) * ones, jnp.exp((c - 1.0 - i) * log_gamma) * ones]))
    return jnp.stack(tabs)


def _retention_chunk(cos_ref, sin_ref, tab_ref, p_ref, o_ref, s_ref):
    c = CHUNK
    cos2 = cos_ref[0]
    sin2 = sin_ref[0]

    def rot(t):
        return t * cos2 + pltpu.roll(t, HEAD_DIM // 2, 1) * sin2

    for h in range(RET_HEADS):
        log_gamma = math.log(1.0 - 2.0 ** (-5.0 - h))
        q = rot(p_ref[0, :, h * HEAD_DIM:(h + 1) * HEAD_DIM])
        k = rot(p_ref[0, :, RET_DIM + h * HEAD_DIM:RET_DIM + (h + 1) * HEAD_DIM]) * (HEAD_DIM ** -0.5)
        v = p_ref[0, :, 2 * RET_DIM + h * HEAD_DIM:2 * RET_DIM + (h + 1) * HEAD_DIM].astype(BF16)
        g = p_ref[0, :, 3 * RET_DIM + h * HEAD_DIM:3 * RET_DIM + (h + 1) * HEAD_DIM]
        scores = _dot_nt(q.astype(BF16), k.astype(BF16)) * tab_ref[h, 0]
        st = s_ref[h]
        y = _dot(scores.astype(BF16), v) + _dot((q * tab_ref[h, 1]).astype(BF16), st.astype(BF16))
        s_ref[h] = math.exp(c * log_gamma) * st + _dot_tn((k * tab_ref[h, 2]).astype(BF16), v)
        o = y * lax.rsqrt(jnp.mean(y * y, axis=-1, keepdims=True) + RMS_EPS)
        o_ref[0, :, h * HEAD_DIM:(h + 1) * HEAD_DIM] = (_silu(g) * o).astype(BF16)
        yield


def _hgrn2_chunk(layer, p_ref, lbl_ref, nw_ref, tri_ref, o_ref, s_ref):
    c = CHUNK
    lbl = lbl_ref[...]
    e = jnp.exp(lbl - jnp.max(lbl, axis=0, keepdims=True))
    p = e / jnp.sum(e, axis=0, keepdims=True)
    lb = jnp.zeros((1, HG_DIM), F32)
    for d in range(1, layer + 1):
        lb = lb + p[d:d + 1, :]

    q = p_ref[0, :, 0:HG_DIM]
    f = lb + (1.0 - lb) * jax.nn.sigmoid(p_ref[0, :, HG_DIM:2 * HG_DIM])
    log_f = jnp.log(f)
    k = 1.0 - f
    v_bf = p_ref[0, :, 2 * HG_DIM:3 * HG_DIM].astype(BF16)
    q_bf = q.astype(BF16)
    k_bf = k.astype(BF16)

    cum = _dot_split_rhs(tri_ref[...], log_f, 3)
    row = lax.broadcasted_iota(jnp.int32, (c, 1), 0)
    col = lax.broadcasted_iota(jnp.int32, (1, c), 1)
    r_sub = row & (SUBLANES - 1)
    yield

    def pick_row(x, block, j):
        x3 = x.reshape(c // block, block, HG_DIM)[:, j:j + 1, :]
        return jnp.broadcast_to(x3, (c // block, block, HG_DIM)).reshape(c, HG_DIM)

    merged, masks = [], []
    for b in HG_LEVELS:
        sh = int(math.log2(b))
        upper = ((row >> sh) & 1) == 1
        if b == 1:
            m = jnp.where(upper, q * f, k)
        else:
            if 2 * b >= SUBLANES:
                ref = pick_row(cum, 2 * b, b - 1)
            else:
                ref = jnp.where(r_sub < 2 * b, pick_row(cum, SUBLANES, b - 1), pick_row(cum, SUBLANES, 3 * b - 1))
            m = jnp.where(upper, q, k) * jnp.exp2((cum - ref) * jnp.where(upper, LOG2E, -LOG2E))
        merged.append(m.astype(BF16))
        masks.append(((row >> (sh + 1)) == (col >> (sh + 1))) & upper & (((col >> sh) & 1) == 0))
        yield

    tail = cum[c - 1:c, :] - cum
    q_in = (q * jnp.exp(cum)).astype(BF16)
    k_st = (k * jnp.exp(tail)).astype(BF16)
    e_tot = jnp.exp(cum[c - 1:c, :])

    outs = []
    for h in range(HG_HEADS):
        sl = slice(h * HEAD_DIM, (h + 1) * HEAD_DIM)
        scores = jnp.where(row == col, _dot_nt(q_bf[:, sl], k_bf[:, sl]), 0.0)
        for lv in range(len(HG_LEVELS)):
            m_h = merged[lv][:, sl]
            scores = jnp.where(masks[lv], _dot_nt(m_h, m_h), scores)
        st_t = s_ref[h]
        y = _dot(scores.astype(BF16), v_bf[:, sl]) + _dot_nt(q_in[:, sl], st_t.astype(BF16))
        s_ref[h] = st_t * e_tot[:, sl] + _dot_tn(v_bf[:, sl], k_st[:, sl])
        outs.append(y * lax.rsqrt(jnp.mean(y * y, axis=-1, keepdims=True) + RMS_EPS))
        yield
    o = jnp.concatenate(outs, axis=1) * nw_ref[0] * _silu(p_ref[0, :, 3 * HG_DIM:4 * HG_DIM])
    o_ref[0, :, RET_DIM:RET_DIM + HG_DIM] = o.astype(BF16)


def _mamba2_chunk(zx_ref, bc_ref, dt_ref, cw_ref, cb_ref, dtb_ref, alog_ref, dskip_ref, nw_ref, tri_ref, expand_ref,
                  o_ref, s_ref, ext_x_ref, ext_b_ref):
    c = CHUNK
    pad = SUBLANES

    r_sub = lax.broadcasted_iota(jnp.int32, (pad, 1), 0)

    def conv_silu(tail_ref, u, lo, hi):
        prev = tail_ref[...]
        acc = cb_ref[0, :, lo:hi] + cw_ref[0, M_CONV - 1:M_CONV, lo:hi] * u
        for s in range(1, M_CONV):
            rolled = pltpu.roll(u, s, 0)
            first = jnp.where(r_sub < s, pltpu.roll(prev, s, 0), rolled[0:pad])
            acc = acc + cw_ref[0, M_CONV - 1 - s:M_CONV - s, lo:hi] * jnp.concatenate([first, rolled[pad:]], axis=0)
        tail_ref[...] = u[c - pad:c]
        return _silu(acc)

    xs = conv_silu(ext_x_ref, zx_ref[0, :, M_DIM:], 0, M_DIM)
    yield
    bc = conv_silu(ext_b_ref, bc_ref[0], M_DIM, M_CONV_DIM)
    yield

    dt = jax.nn.softplus(dt_ref[0] + dtb_ref[0])
    a = -jnp.exp(alog_ref[0])
    cum = _dot_split_rhs(tri_ref[...], dt * a, 3)
    cum_t = cum.T
    e_cum = jnp.exp(cum)
    w_state = dt * jnp.exp(cum[c - 1:c, :] - cum)
    spread = _dot_split_lhs(jnp.concatenate([dt, e_cum, w_state], axis=0), expand_ref[...], 2)
    dt_e, e_cum_e, w_state_e = spread[0:c], spread[c:2 * c], spread[2 * c:3 * c]
    x_dt = (xs * dt_e).astype(BF16)
    x_st = (xs * w_state_e).astype(BF16)
    yield

    row = lax.broadcasted_iota(jnp.int32, (c, 1), 0)
    col = lax.broadcasted_iota(jnp.int32, (1, c), 1)
    causal = row >= col
    lane = lax.broadcasted_iota(jnp.int32, (1, LANES), 1)
    gw = M_DIM // M_GROUPS
    heads_per_group = M_HEADS // M_GROUPS

    ys = []
    for g in range(M_GROUPS):
        b_g = bc[:, g * M_STATE:(g + 1) * M_STATE].astype(BF16)
        c_g = bc[:, (M_GROUPS + g) * M_STATE:(M_GROUPS + g + 1) * M_STATE].astype(BF16)
        gram = _dot_nt(c_g, b_g)
        st = s_ref[g]
        gsl = slice(g * gw, (g + 1) * gw)
        y_g = _dot(c_g, st.astype(BF16)) * e_cum_e[:, gsl]
        s_ref[g] = st * e_cum_e[c - 1:c, gsl] + _dot_tn(b_g, x_st[:, gsl])
        pieces = []
        for pr in range(heads_per_group // 2):
            psl = slice(g * gw + pr * LANES, g * gw + (pr + 1) * LANES)
            x_pair = x_dt[:, psl]
            sc, xm = [], []
            for half in range(2):
                h = g * heads_per_group + 2 * pr + half
                seg = cum[:, h:h + 1] - cum_t[h:h + 1, :]
                decay = jnp.exp(jnp.where(causal, seg, -jnp.inf))
                in_head = (lane >= half * M_HEAD_DIM) & (lane < (half + 1) * M_HEAD_DIM)
                sc.append((gram * decay).astype(BF16))
                xm.append(jnp.where(in_head, x_pair, jnp.zeros_like(x_pair)))
            pieces.append(_dot(jnp.concatenate(sc, axis=1), jnp.concatenate(xm, axis=0)))
            yield
        ys.append(y_g + jnp.concatenate(pieces, axis=1))
    y = jnp.concatenate(ys, axis=1) + xs * dskip_ref[0]
    y = y * _silu(zx_ref[0, :, 0:M_DIM])
    outs = []
    for g in range(M_GROUPS):
        yg = y[:, g * gw:(g + 1) * gw]
        outs.append(yg * lax.rsqrt(jnp.mean(yg * yg, axis=-1, keepdims=True) + RMS_EPS))
    o_ref[0, :, RET_DIM + HG_DIM:] = (jnp.concatenate(outs, axis=1) * nw_ref[0]).astype(BF16)


def _mixers_kernel(layer, has_next, n_chunks, *refs):
    (cos_ref, sin_ref, rtab_ref, ret_ref, hg_ref, zx_ref, bc_ref, dt_ref, lbl_ref, hnw_ref, cw_ref, cb_ref, dtb_ref,
     alog_ref, dskip_ref, mnw_ref, tri_ref, expand_ref, wout_ref, w1_ref, w2_ref) = refs[:21]
    refs = refs[21:]
    if has_next:
        win_ref, refs = refs[0], refs[1:]
    o_ref, wout_o_ref, w1_o_ref, w2_o_ref = refs[:4]
    refs = refs[4:]
    if has_next:
        win_o_ref, refs = refs[0], refs[1:]
    ret_s_ref, hg_s_ref, m_s_ref, ext_x_ref, ext_b_ref = refs

    wout_o_ref[...] = wout_ref[...].astype(BF16)
    w1_o_ref[...] = w1_ref[...].astype(BF16)
    w2_o_ref[...] = w2_ref[...].astype(BF16)
    if has_next:
        _win_convert(win_ref, win_o_ref, pl.program_id(0) * n_chunks + pl.program_id(1))

    @pl.when(pl.program_id(1) == 0)
    def _():
        ret_s_ref[...] = jnp.zeros_like(ret_s_ref)
        hg_s_ref[...] = jnp.zeros_like(hg_s_ref)
        m_s_ref[...] = jnp.zeros_like(m_s_ref)
        ext_x_ref[...] = jnp.zeros_like(ext_x_ref)
        ext_b_ref[...] = jnp.zeros_like(ext_b_ref)

    stages = [_hgrn2_chunk(layer, hg_ref, lbl_ref, hnw_ref, tri_ref, o_ref, hg_s_ref),
              _mamba2_chunk(zx_ref, bc_ref, dt_ref, cw_ref, cb_ref, dtb_ref, alog_ref, dskip_ref, mnw_ref, tri_ref,
                            expand_ref, o_ref, m_s_ref, ext_x_ref, ext_b_ref),
              _retention_chunk(cos_ref, sin_ref, rtab_ref, ret_ref, o_ref, ret_s_ref)]
    while stages:
        for stage in list(stages):
            if next(stage, "done") == "done":
                stages.remove(stage)


def _mixers(proj, cos2, sin2, ret_tabs, lb_logits, hg_norm_w3, conv_w, conv_b3, dtb3, alog3, dskip3, m_norm_w3, tri,
            expand, w_out, w1, w2, w_in_t, layer):
    bsz, seq, _ = proj.shape
    c = CHUNK
    n_chunks = seq // c
    n_steps = bsz * n_chunks
    has_next = layer + 1 < DEPTH

    def cols(first, width):
        assert first % width == 0
        return pl.BlockSpec((1, c, width), lambda b, i: (b, i, first // width))

    def lay(shape):
        return pl.BlockSpec((1,) + shape, lambda b, i: (layer, 0, 0))

    def const(shape):
        return pl.BlockSpec(shape, lambda b, i: (0,) * len(shape))

    def slab_in(w):
        rows = w.shape[1] // n_steps
        assert rows * n_steps == w.shape[1] and rows % (2 * SUBLANES) == 0
        return pl.BlockSpec((None, rows, w.shape[2]), lambda b, i: (layer, b * n_chunks + i, 0))

    def slab_out(w):
        rows = w.shape[1] // n_steps
        return (pl.BlockSpec((rows, w.shape[2]), lambda b, i: (b * n_chunks + i, 0)),
                jax.ShapeDtypeStruct(w.shape[1:], BF16))

    tab = pl.BlockSpec((1, c, HEAD_DIM), lambda b, i: (b, i, 0))
    in_specs = [
        tab, tab, const((RET_HEADS, 3, c, c)),
        cols(*COL_RET), cols(*COL_HG), cols(*COL_ZX), cols(*COL_BC), cols(*COL_DT),
        const((DEPTH, HG_DIM)), lay((1, HG_DIM)),
        lay((M_CONV, M_CONV_DIM)), lay((1, M_CONV_DIM)), lay((1, LANES)), lay((1, LANES)), lay((1, M_DIM)),
        lay((1, M_DIM)), const((c, c)), const((LANES, M_DIM)),
        slab_in(w_out), slab_in(w1), slab_in(w2),
    ]
    args = [cos2, sin2, ret_tabs, proj, proj, proj, proj, proj, lb_logits, hg_norm_w3, conv_w, conv_b3, dtb3, alog3,
            dskip3, m_norm_w3, tri, expand, w_out, w1, w2]
    out_specs = [pl.BlockSpec((1, c, D_MODEL), lambda b, i: (b, i, 0))]
    out_shape = [jax.ShapeDtypeStruct((bsz, seq, D_MODEL), BF16)]
    for w in (w_out, w1, w2):
        spec, shape = slab_out(w)
        out_specs.append(spec)
        out_shape.append(shape)
    if has_next:
        win_in, win_out, win_shape = _win_specs(layer + 1, n_steps, lambda b, i: b * n_chunks + i)
        in_specs.append(win_in)
        args.append(w_in_t)
        out_specs.append(win_out)
        out_shape.append(win_shape)
    return pl.pallas_call(
        functools.partial(_mixers_kernel, layer, has_next, n_chunks),
        grid=(bsz, n_chunks),
        in_specs=in_specs,
        out_specs=out_specs,
        out_shape=out_shape,
        scratch_shapes=[
            pltpu.VMEM((RET_HEADS, HEAD_DIM, HEAD_DIM), F32),
            pltpu.VMEM((HG_HEADS, HEAD_DIM, HEAD_DIM), F32),
            pltpu.VMEM((M_GROUPS, M_STATE, M_DIM // M_GROUPS), F32),
            pltpu.VMEM((SUBLANES, M_DIM), F32),
            pltpu.VMEM((SUBLANES, M_BC), F32),
        ],
        compiler_params=_params("arbitrary", "arbitrary"),
        name="mixers",
    )(*args)


def _outproj_kernel(x_ref, mix_ref, w_ref, ada_ref, g_ref, b_ref, o_ref):
    gate = ada_ref[0, 2:3, :]
    for r in range(0, o_ref.shape[1], OUTPROJ_TR):
        rs = slice(r, r + OUTPROJ_TR)
        y = _dot(mix_ref[0, rs, :], w_ref[...])
        o_ref[0, rs, :] = _layer_norm(DEEPNORM_ALPHA * x_ref[0, rs, :] + (1.0 + gate) * y,
                                      g_ref[0, 0:1, :], b_ref[0, 0:1, :])


def _outproj(x, mix, w_out_b, ada, ln_g, ln_b, layer):
    bsz, seq, _ = x.shape
    tm = min(OUTPROJ_TM, seq)
    rows = pl.BlockSpec((1, tm, D_MODEL), lambda b, m: (b, m, 0))
    return pl.pallas_call(
        _outproj_kernel,
        grid=(bsz, seq // tm),
        in_specs=[
            rows, rows,
            pl.BlockSpec((D_MODEL, D_MODEL), lambda b, m: (0, 0)),
            pl.BlockSpec((1, N_ADA, D_MODEL), lambda b, m: (layer * bsz + b, 0, 0)),
            pl.BlockSpec((1, 2, D_MODEL), lambda b, m: (layer, 0, 0)),
            pl.BlockSpec((1, 2, D_MODEL), lambda b, m: (layer, 0, 0)),
        ],
        out_specs=rows,
        out_shape=jax.ShapeDtypeStruct(x.shape, F32),
        compiler_params=_params("arbitrary", "arbitrary"),
        name="outproj",
    )(x, mix, w_out_b, ada, ln_g, ln_b)


def _mlp_kernel(x_ref, ada_ref, w1_ref, w2_ref, g_ref, b_ref, o_ref, h_ref):
    f = pl.program_id(2)
    last_f = pl.num_programs(2) - 1
    blocks = [slice(r, r + MLP_TR) for r in range(0, o_ref.shape[1], MLP_TR)]

    def step(first, last):
        u2 = []
        for rs in blocks:
            if first:
                h_ref[rs, :] = (x_ref[0, rs, :] * (1.0 + ada_ref[0, 4:5, :]) + ada_ref[0, 3:4, :]).astype(BF16)
            u = jnp.maximum(_dot(h_ref[rs, :], w1_ref[...]), 0.0)
            u2.append((u * u).astype(BF16))
        for rs, u2_r in zip(blocks, u2):
            for n in range(0, D_MODEL, MLP_TN):
                y = _dot(u2_r, w2_ref[:, n:n + MLP_TN])
                if first:
                    o_ref[0, rs, n:n + MLP_TN] = y
                else:
                    o_ref[0, rs, n:n + MLP_TN] += y
            if last:
                for r in range(rs.start, rs.stop, LN_ROWS):
                    ls = slice(r, r + LN_ROWS)
                    o_ref[0, ls, :] = _layer_norm(
                        DEEPNORM_ALPHA * x_ref[0, ls, :] + (1.0 + ada_ref[0, 5:6, :]) * o_ref[0, ls, :],
                        g_ref[0, 1:2, :], b_ref[0, 1:2, :])

    pl.when(f == 0)(functools.partial(step, True, False))
    pl.when((f > 0) & (f < last_f))(functools.partial(step, False, False))
    pl.when(f == last_f)(functools.partial(step, False, True))


def _mlp(x, ada, w1_b, w2_b, ln_g, ln_b, layer):
    bsz, seq, _ = x.shape
    tm = min(MLP_TM, seq)
    tf = MLP_TF
    return pl.pallas_call(
        _mlp_kernel,
        grid=(bsz, seq // tm, D_FF // tf),
        in_specs=[
            pl.BlockSpec((1, tm, D_MODEL), lambda b, m, f: (b, m, 0)),
            pl.BlockSpec((1, N_ADA, D_MODEL), lambda b, m, f: (layer * bsz + b, 0, 0)),
            pl.BlockSpec((D_MODEL, tf), lambda b, m, f: (0, f)),
            pl.BlockSpec((tf, D_MODEL), lambda b, m, f: (f, 0)),
            pl.BlockSpec((1, 2, D_MODEL), lambda b, m, f: (layer, 0, 0)),
            pl.BlockSpec((1, 2, D_MODEL), lambda b, m, f: (layer, 0, 0)),
        ],
        out_specs=pl.BlockSpec((1, tm, D_MODEL), lambda b, m, f: (b, m, 0)),
        out_shape=jax.ShapeDtypeStruct(x.shape, F32),
        scratch_shapes=[pltpu.VMEM((tm, D_MODEL), BF16)],
        compiler_params=_params("arbitrary", "arbitrary", "arbitrary"),
        name="mlp",
    )(x, ada, w1_b, w2_b, ln_g, ln_b)


def kernel(x, c, positions, lb_logits, w_in, w_out, w_ada, b_ada, ln_g, ln_b, hg_norm_w, m_conv_w, m_conv_b,
           m_dt_bias, m_a_log, m_d, m_norm_w, w1, w2):
    bsz, seq, _ = x.shape
    assert seq % CHUNK == 0

    pad_heads = ((0, 0), (0, LANES - M_HEADS))
    dtb3 = jnp.pad(m_dt_bias, pad_heads).reshape(DEPTH, 1, LANES)
    alog3 = jnp.pad(m_a_log, pad_heads).reshape(DEPTH, 1, LANES)
    dskip3 = jnp.repeat(m_d, M_HEAD_DIM, axis=1).reshape(DEPTH, 1, M_DIM)
    conv_b3 = m_conv_b.reshape(DEPTH, 1, M_CONV_DIM)
    m_norm_w3 = m_norm_w.reshape(DEPTH, 1, M_DIM)
    hg_norm_w3 = hg_norm_w.reshape(DEPTH, 1, HG_DIM)
    pos3 = positions.reshape(bsz, seq, 1)

    inv_freq = 1.0 / (ROPE_BASE ** jnp.linspace(0.0, 1.0, HEAD_DIM // 2, dtype=F32))
    invf2 = jnp.concatenate([inv_freq, inv_freq]).reshape(1, HEAD_DIM)
    idx = jnp.arange(CHUNK)
    tri = (idx[None, :] <= idx[:, None]).astype(BF16)
    expand = (jnp.arange(M_DIM)[None, :] // M_HEAD_DIM == jnp.arange(LANES)[:, None]).astype(BF16)
    ret_tabs = _retention_tables()

    w_in_t = jnp.swapaxes(w_in, 1, 2)
    ada = _ada(c, w_ada, b_ada).reshape(DEPTH * bsz, N_ADA, D_MODEL)
    cos2, sin2, w_in_b = _rope(pos3, invf2, w_in_t)
    for l in range(DEPTH):
        proj = _inproj(x, ada, w_in_b, l)
        mix, w_out_b, w1_b, w2_b, *rest = _mixers(proj, cos2, sin2, ret_tabs, lb_logits, hg_norm_w3, m_conv_w,
                                                  conv_b3, dtb3, alog3, dskip3, m_norm_w3, tri, expand, w_out, w1,
                                                  w2, w_in_t, l)
        if rest:
            w_in_b = rest[0]
        x = _outproj(x, mix, w_out_b, ada, ln_g, ln_b, l)
        x = _mlp(x, ada, w1_b, w2_b, ln_g, ln_b, l)
    return x
```

```python
import functools
import math

import jax
import jax.numpy as jnp
from jax import lax
from jax.experimental import pallas as pl
from jax.experimental.pallas import tpu as pltpu

F32 = jnp.float32
BF16 = jnp.bfloat16

D_MODEL = 2048
DEPTH = 4
RET_HEADS = 4
RET_DIM = 512
HEAD_DIM = 128
ROPE_BASE = 10000.0
HG_HEADS = 4
HG_DIM = 512
M_DIM = 1024
M_HEAD_DIM = 64
M_HEADS = 16
M_GROUPS = 2
M_STATE = 128
M_CONV = 4
M_CONV_DIM = 1536
M_BC = M_CONV_DIM - M_DIM
D_FF = 4 * D_MODEL
IN_COLS = 6672
DEEPNORM_ALPHA = (2 * DEPTH) ** 0.25
LN_EPS = 1e-5
RMS_EPS = 1e-6
N_ADA = 6
LOG2E = math.log2(math.e)

LANES = 128
SUBLANES = 8
IN_COLS_PAD = 6912
VMEM_LIMIT = 56 * 2 ** 20

COL_RET = (0, 4 * RET_DIM)
COL_HG = (4 * RET_DIM, 4 * HG_DIM)
COL_ZX = (4 * RET_DIM + 4 * HG_DIM, 2 * M_DIM)
COL_BC = (COL_ZX[0] + 2 * M_DIM, M_BC)
COL_DT = (COL_BC[0] + M_BC, LANES)

ADA_TN = 1024
ROPE_TM = 512
INPROJ_TM, INPROJ_TN = 256, 768
OUTPROJ_TM = 512
OUTPROJ_TR = 256
MLP_TM, MLP_TF = 1024, 512
MLP_TN = 512
MLP_TR = 256

CHUNK = 128
HG_LEVELS = tuple(1 << i for i in range(int(math.log2(CHUNK))))


def _params(*sem):
    return pltpu.CompilerParams(dimension_semantics=sem, vmem_limit_bytes=VMEM_LIMIT)


def _silu(x):
    h = 0.5 * x
    return h + h * jnp.tanh(h)


def _dot(a, b):
    return jnp.dot(a, b, preferred_element_type=F32)


def _dot_nt(a, b):
    return lax.dot_general(a, b, (((1,), (1,)), ((), ())), preferred_element_type=F32)


def _dot_tn(a, b):
    return lax.dot_general(a, b, (((0,), (0,)), ((), ())), preferred_element_type=F32)


def _split(x, pieces):
    out = []
    for _ in range(pieces - 1):
        p = x.astype(BF16)
        out.append(p)
        x = x - p.astype(F32)
    out.append(x.astype(BF16))
    return out


def _dot_split_rhs(m_bf16, x, pieces):
    return functools.reduce(jnp.add, [_dot(m_bf16, p) for p in _split(x, pieces)])


def _dot_split_lhs(x, m_bf16, pieces):
    return functools.reduce(jnp.add, [_dot(p, m_bf16) for p in _split(x, pieces)])


def _layer_norm(v, g, b):
    mu = jnp.mean(v, axis=-1, keepdims=True)
    d = v - mu
    var = jnp.mean(d * d, axis=-1, keepdims=True)
    return d * lax.rsqrt(var + LN_EPS) * g + b


def _win_slab_rows(n_steps):
    return LANES * pl.cdiv(IN_COLS_PAD, LANES * n_steps)


def _win_specs(layer, n_steps, step_of):
    rows = _win_slab_rows(n_steps)
    last_in = pl.cdiv(IN_COLS, rows) - 1
    last_out = pl.cdiv(IN_COLS_PAD, rows) - 1
    assert last_out < n_steps
    return (pl.BlockSpec((None, rows, D_MODEL), lambda *g: (layer, jnp.minimum(step_of(*g), last_in), 0)),
            pl.BlockSpec((rows, D_MODEL), lambda *g: (jnp.minimum(step_of(*g), last_out), 0)),
            jax.ShapeDtypeStruct(((last_out + 1) * rows, D_MODEL), BF16))


def _win_convert(win_ref, win_o_ref, step):
    rows = win_o_ref.shape[0]
    blk = jnp.minimum(step, pl.cdiv(IN_COLS_PAD, rows) - 1)
    n_idx = blk * rows + lax.broadcasted_iota(jnp.int32, (rows, 1), 0)
    win_o_ref[...] = jnp.where(n_idx < IN_COLS, win_ref[...], 0.0).astype(BF16)


def _ada_kernel(c_ref, w_ref, b_ref, o_ref):
    cond = _silu(c_ref[...]).astype(BF16)
    o_ref[0] = _dot(cond, w_ref[0].astype(BF16)) + b_ref[0]


def _ada(c, w_ada, b_ada):
    bsz = c.shape[0]
    tn = ADA_TN
    return pl.pallas_call(
        _ada_kernel,
        grid=(DEPTH, N_ADA * D_MODEL // tn),
        in_specs=[
            pl.BlockSpec((bsz, D_MODEL), lambda l, n: (0, 0)),
            pl.BlockSpec((1, D_MODEL, tn), lambda l, n: (l, 0, n)),
            pl.BlockSpec((1, 1, tn), lambda l, n: (l, 0, n)),
        ],
        out_specs=pl.BlockSpec((1, bsz, tn), lambda l, n: (l, 0, n)),
        out_shape=jax.ShapeDtypeStruct((DEPTH, bsz, N_ADA * D_MODEL), F32),
        compiler_params=_params("arbitrary", "arbitrary"),
        name="ada",
    )(c, w_ada, b_ada.reshape(DEPTH, 1, N_ADA * D_MODEL))


def _rope_kernel(n_tiles, pos_ref, invf_ref, win_ref, cos_ref, sin_ref, win_o_ref):
    _win_convert(win_ref, win_o_ref, pl.program_id(0) * n_tiles + pl.program_id(1))
    ang = pos_ref[0].astype(F32) * invf_ref[...]
    lane = lax.broadcasted_iota(jnp.int32, (1, HEAD_DIM), 1)
    cos_ref[0] = jnp.cos(ang)
    sin_ref[0] = jnp.where(lane < HEAD_DIM // 2, -1.0, 1.0) * jnp.sin(ang)


def _rope(pos3, invf2, w_in_t):
    bsz, seq, _ = pos3.shape
    tm = min(ROPE_TM, seq)
    n_tiles = seq // tm
    tab = pl.BlockSpec((1, tm, HEAD_DIM), lambda b, m: (b, m, 0))
    shape = jax.ShapeDtypeStruct((bsz, seq, HEAD_DIM), F32)
    win_in, win_out, win_shape = _win_specs(0, bsz * n_tiles, lambda b, m: b * n_tiles + m)
    return pl.pallas_call(
        functools.partial(_rope_kernel, n_tiles),
        grid=(bsz, n_tiles),
        in_specs=[pl.BlockSpec((1, tm, 1), lambda b, m: (b, m, 0)),
                  pl.BlockSpec((1, HEAD_DIM), lambda b, m: (0, 0)),
                  win_in],
        out_specs=(tab, tab, win_out),
        out_shape=(shape, shape, win_shape),
        compiler_params=_params("arbitrary", "arbitrary"),
        name="rope",
    )(pos3, invf2, w_in_t)


def _inproj_kernel(x_ref, ada_ref, wt_ref, o_ref):
    sh = ada_ref[0, 0:1, :]
    sc = ada_ref[0, 1:2, :]
    h = (x_ref[0] * (1.0 + sc) + sh).astype(BF16)
    for n in range(0, IN_COLS_PAD, INPROJ_TN):
        o_ref[0, :, n:n + INPROJ_TN] = _dot_nt(h, wt_ref[n:n + INPROJ_TN, :])


def _inproj(x, ada, w_in_t, layer):
    bsz, seq, _ = x.shape
    tm = min(INPROJ_TM, seq)
    return pl.pallas_call(
        _inproj_kernel,
        grid=(bsz, seq // tm),
        in_specs=[
            pl.BlockSpec((1, tm, D_MODEL), lambda b, m: (b, m, 0)),
            pl.BlockSpec((1, N_ADA, D_MODEL), lambda b, m: (layer * bsz + b, 0, 0)),
            pl.BlockSpec((IN_COLS_PAD, D_MODEL), lambda b, m: (0, 0), pipeline_mode=pl.Buffered(1)),
        ],
        out_specs=pl.BlockSpec((1, tm, IN_COLS_PAD), lambda b, m: (b, m, 0)),
        out_shape=jax.ShapeDtypeStruct((bsz, seq, IN_COLS_PAD), F32),
        compiler_params=_params("arbitrary", "arbitrary"),
        name="inproj",
    )(x, ada, w_in_t)


def _retention_tables():
    c = CHUNK
    i = jnp.arange(c, dtype=F32)[:, None]
    j = jnp.arange(c, dtype=F32)[None, :]
    ones = jnp.ones((1, HEAD_DIM), F32)
    tabs = []
    for h in range(RET_HEADS):
        log_gamma = math.log(1.0 - 2.0 ** (-5.0 - h))
        decay = jnp.where(i >= j, jnp.exp((i - j) * log_gamma), 0.0)
        tabs.append(jnp.stack([decay, jnp.exp((i + 1.0) * log_gamma) * ones, jnp.exp((c - 1.0 - i) * log_gamma) * ones]))
    return jnp.stack(tabs)


def _retention_chunk(cos_ref, sin_ref, tab_ref, p_ref, o_ref, s_ref):
    c = CHUNK
    cos2 = cos_ref[0]
    sin2 = sin_ref[0]

    def rot(t):
        return t * cos2 + pltpu.roll(t, HEAD_DIM // 2, 1) * sin2

    for h in range(RET_HEADS):
        log_gamma = math.log(1.0 - 2.0 ** (-5.0 - h))
        q = rot(p_ref[0, :, h * HEAD_DIM:(h + 1) * HEAD_DIM])
        k = rot(p_ref[0, :, RET_DIM + h * HEAD_DIM:RET_DIM + (h + 1) * HEAD_DIM]) * (HEAD_DIM ** -0.5)
        v = p_ref[0, :, 2 * RET_DIM + h * HEAD_DIM:2 * RET_DIM + (h + 1) * HEAD_DIM].astype(BF16)
        g = p_ref[0, :, 3 * RET_DIM + h * HEAD_DIM:3 * RET_DIM + (h + 1) * HEAD_DIM]
        scores = _dot_nt(q.astype(BF16), k.astype(BF16)) * tab_ref[h, 0]
        st = s_ref[h]
        y = _dot(scores.astype(BF16), v) + _dot((q * tab_ref[h, 1]).astype(BF16), st.astype(BF16))
        s_ref[h] = math.exp(c * log_gamma) * st + _dot_tn((k * tab_ref[h, 2]).astype(BF16), v)
        o = y * lax.rsqrt(jnp.mean(y * y, axis=-1, keepdims=True) + RMS_EPS)
        o_ref[0, :, h * HEAD_DIM:(h + 1) * HEAD_DIM] = (_silu(g) * o).astype(BF16)
        yield


def _hgrn2_chunk(layer, p_ref, lbl_ref, nw_ref, tri_ref, o_ref, s_ref):
    c = CHUNK
    lbl = lbl_ref[...]
    e = jnp.exp(lbl - jnp.max(lbl, axis=0, keepdims=True))
    p = e / jnp.sum(e, axis=0, keepdims=True)
    lb = jnp.zeros((1, HG_DIM), F32)
    for d in range(1, layer + 1):
        lb = lb + p[d:d + 1, :]

    q = p_ref[0, :, 0:HG_DIM]
    f = lb + (1.0 - lb) * jax.nn.sigmoid(p_ref[0, :, HG_DIM:2 * HG_DIM])
    log_f = jnp.log(f)
    k = 1.0 - f
    v_bf = p_ref[0, :, 2 * HG_DIM:3 * HG_DIM].astype(BF16)
    q_bf = q.astype(BF16)
    k_bf = k.astype(BF16)

    cum = _dot_split_rhs(tri_ref[...], log_f, 3)
    row = lax.broadcasted_iota(jnp.int32, (c, 1), 0)
    col = lax.broadcasted_iota(jnp.int32, (1, c), 1)
    r_sub = row & (SUBLANES - 1)
    yield

    def pick_row(x, block, j):
        x3 = x.reshape(c // block, block, HG_DIM)[:, j:j + 1, :]
        return jnp.broadcast_to(x3, (c // block, block, HG_DIM)).reshape(c, HG_DIM)

    merged, masks = [], []
    for b in HG_LEVELS:
        sh = int(math.log2(b))
        upper = ((row >> sh) & 1) == 1
        if b == 1:
            m = jnp.where(upper, q * f, k)
        else:
            if 2 * b >= SUBLANES:
                ref = pick_row(cum, 2 * b, b - 1)
            else:
                ref = jnp.where(r_sub < 2 * b, pick_row(cum, SUBLANES, b - 1), pick_row(cum, SUBLANES, 3 * b - 1))
            m = jnp.where(upper, q, k) * jnp.exp2((cum - ref) * jnp.where(upper, LOG2E, -LOG2E))
        merged.append(m.astype(BF16))
        masks.append(((row >> (sh + 1)) == (col >> (sh + 1))) & upper & (((col >> sh) & 1) == 0))
        yield

    tail = cum[c - 1:c, :] - cum
    q_in = (q * jnp.exp(cum)).astype(BF16)
    k_st = (k * jnp.exp(tail)).astype(BF16)
    e_tot = jnp.exp(cum[c - 1:c, :])

    outs = []
    for h in range(HG_HEADS):
        sl = slice(h * HEAD_DIM, (h + 1) * HEAD_DIM)
        scores = jnp.where(row == col, _dot_nt(q_bf[:, sl], k_bf[:, sl]), 0.0)
        for lv in range(len(HG_LEVELS)):
            m_h = merged[lv][:, sl]
            scores = jnp.where(masks[lv], _dot_nt(m_h, m_h), scores)
        st_t = s_ref[h]
        y = _dot(scores.astype(BF16), v_bf[:, sl]) + _dot_nt(q_in[:, sl], st_t.astype(BF16))
        s_ref[h] = st_t * e_tot[:, sl] + _dot_tn(v_bf[:, sl], k_st[:, sl])
        outs.append(y * lax.rsqrt(jnp.mean(y * y, axis=-1, keepdims=True) + RMS_EPS))
        yield
    o = jnp.concatenate(outs, axis=1) * nw_ref[0] * _silu(p_ref[0, :, 3 * HG_DIM:4 * HG_DIM])
    o_ref[0, :, RET_DIM:RET_DIM + HG_DIM] = o.astype(BF16)


def _mamba2_chunk(zx_ref, bc_ref, dt_ref, cw_ref, cb_ref, dtb_ref, alog_ref, dskip_ref, nw_ref, tri_ref, expand_ref,
                  o_ref, s_ref, tail_x_ref, tail_b_ref):
    c = CHUNK
    pad = SUBLANES

    r_sub = lax.broadcasted_iota(jnp.int32, (pad, 1), 0)

    def conv_silu(tail_ref, u, lo, hi):
        prev = tail_ref[...]
        acc = cb_ref[0, :, lo:hi] + cw_ref[0, M_CONV - 1:M_CONV, lo:hi] * u
        for s in range(1, M_CONV):
            rolled = pltpu.roll(u, s, 0)
            first = jnp.where(r_sub < s, pltpu.roll(prev, s, 0), rolled[0:pad])
            acc = acc + cw_ref[0, M_CONV - 1 - s:M_CONV - s, lo:hi] * jnp.concatenate([first, rolled[pad:]], axis=0)
        tail_ref[...] = u[c - pad:c]
        return _silu(acc)

    xs = conv_silu(tail_x_ref, zx_ref[0, :, M_DIM:], 0, M_DIM)
    yield
    bc = conv_silu(tail_b_ref, bc_ref[0], M_DIM, M_CONV_DIM)
    yield

    dt = jax.nn.softplus(dt_ref[0] + dtb_ref[0])
    a = -jnp.exp(alog_ref[0])
    cum = _dot_split_rhs(tri_ref[...], dt * a, 3)
    cum_t = cum.T
    e_cum = jnp.exp(cum)
    w_state = dt * jnp.exp(cum[c - 1:c, :] - cum)
    spread = _dot_split_lhs(jnp.concatenate([dt, e_cum, w_state], axis=0), expand_ref[...], 2)
    dt_e, e_cum_e, w_state_e = spread[0:c], spread[c:2 * c], spread[2 * c:3 * c]
    x_dt = (xs * dt_e).astype(BF16)
    x_st = (xs * w_state_e).astype(BF16)
    yield

    row = lax.broadcasted_iota(jnp.int32, (c, 1), 0)
    col = lax.broadcasted_iota(jnp.int32, (1, c), 1)
    causal = row >= col
    lane = lax.broadcasted_iota(jnp.int32, (1, LANES), 1)
    gw = M_DIM // M_GROUPS
    heads_per_group = M_HEADS // M_GROUPS

    ys = []
    for g in range(M_GROUPS):
        b_g = bc[:, g * M_STATE:(g + 1) * M_STATE].astype(BF16)
        c_g = bc[:, (M_GROUPS + g) * M_STATE:(M_GROUPS + g + 1) * M_STATE].astype(BF16)
        gram = _dot_nt(c_g, b_g)
        st = s_ref[g]
        gsl = slice(g * gw, (g + 1) * gw)
        y_g = _dot(c_g, st.astype(BF16)) * e_cum_e[:, gsl]
        s_ref[g] = st * e_cum_e[c - 1:c, gsl] + _dot_tn(b_g, x_st[:, gsl])
        pieces = []
        for pr in range(heads_per_group // 2):
            psl = slice(g * gw + pr * LANES, g * gw + (pr + 1) * LANES)
            x_pair = x_dt[:, psl]
            sc, xm = [], []
            for half in range(2):
                h = g * heads_per_group + 2 * pr + half
                seg = cum[:, h:h + 1] - cum_t[h:h + 1, :]
                decay = jnp.exp(jnp.where(causal, seg, -jnp.inf))
                in_head = (lane >= half * M_HEAD_DIM) & (lane < (half + 1) * M_HEAD_DIM)
                sc.append((gram * decay).astype(BF16))
                xm.append(jnp.where(in_head, x_pair, jnp.zeros_like(x_pair)))
            pieces.append(_dot(jnp.concatenate(sc, axis=1), jnp.concatenate(xm, axis=0)))
            yield
        ys.append(y_g + jnp.concatenate(pieces, axis=1))
    y = jnp.concatenate(ys, axis=1) + xs * dskip_ref[0]
    y = y * _silu(zx_ref[0, :, 0:M_DIM])
    outs = []
    for g in range(M_GROUPS):
        yg = y[:, g * gw:(g + 1) * gw]
        outs.append(yg * lax.rsqrt(jnp.mean(yg * yg, axis=-1, keepdims=True) + RMS_EPS))
    o_ref[0, :, RET_DIM + HG_DIM:] = (jnp.concatenate(outs, axis=1) * nw_ref[0]).astype(BF16)


def _mixers_kernel(layer, has_next, n_chunks, *refs):
    (cos_ref, sin_ref, rtab_ref, ret_ref, hg_ref, zx_ref, bc_ref, dt_ref, lbl_ref, hnw_ref, cw_ref, cb_ref, dtb_ref,
     alog_ref, dskip_ref, mnw_ref, tri_ref, expand_ref, wout_ref, w1_ref, w2_ref) = refs[:21]
    refs = refs[21:]
    if has_next:
        win_ref, refs = refs[0], refs[1:]
    o_ref, wout_o_ref, w1_o_ref, w2_o_ref = refs[:4]
    refs = refs[4:]
    if has_next:
        win_o_ref, refs = refs[0], refs[1:]
    ret_s_ref, hg_s_ref, m_s_ref, tail_x_ref, tail_b_ref = refs

    wout_o_ref[...] = wout_ref[...].astype(BF16)
    w1_o_ref[...] = w1_ref[...].astype(BF16)
    w2_o_ref[...] = w2_ref[...].astype(BF16)
    if has_next:
        _win_convert(win_ref, win_o_ref, pl.program_id(0) * n_chunks + pl.program_id(1))

    @pl.when(pl.program_id(1) == 0)
    def _():
        ret_s_ref[...] = jnp.zeros_like(ret_s_ref)
        hg_s_ref[...] = jnp.zeros_like(hg_s_ref)
        m_s_ref[...] = jnp.zeros_like(m_s_ref)
        tail_x_ref[...] = jnp.zeros_like(tail_x_ref)
        tail_b_ref[...] = jnp.zeros_like(tail_b_ref)

    stages = [_hgrn2_chunk(layer, hg_ref, lbl_ref, hnw_ref, tri_ref, o_ref, hg_s_ref),
              _mamba2_chunk(zx_ref, bc_ref, dt_ref, cw_ref, cb_ref, dtb_ref, alog_ref, dskip_ref, mnw_ref, tri_ref,
                            expand_ref, o_ref, m_s_ref, tail_x_ref, tail_b_ref),
              _retention_chunk(cos_ref, sin_ref, rtab_ref, ret_ref, o_ref, ret_s_ref)]
    while stages:
        for stage in list(stages):
            if next(stage, "done") == "done":
                stages.remove(stage)


def _mixers(proj, cos2, sin2, ret_tabs, lb_logits, hg_norm_w3, conv_w, conv_b3, dtb3, alog3, dskip3, m_norm_w3, tri,
            expand, w_out, w1, w2, w_in_t, layer):
    bsz, seq, _ = proj.shape
    c = CHUNK
    n_chunks = seq // c
    n_steps = bsz * n_chunks
    has_next = layer + 1 < DEPTH

    def cols(first, width):
        assert first % width == 0
        return pl.BlockSpec((1, c, width), lambda b, i: (b, i, first // width))

    def lay(shape):
        return pl.BlockSpec((1,) + shape, lambda b, i: (layer, 0, 0))

    def const(shape):
        return pl.BlockSpec(shape, lambda b, i: (0,) * len(shape))

    def slab_in(w):
        rows = w.shape[1] // n_steps
        assert rows * n_steps == w.shape[1] and rows % (2 * SUBLANES) == 0
        return pl.BlockSpec((None, rows, w.shape[2]), lambda b, i: (layer, b * n_chunks + i, 0))

    def slab_out(w):
        rows = w.shape[1] // n_steps
        return (pl.BlockSpec((rows, w.shape[2]), lambda b, i: (b * n_chunks + i, 0)),
                jax.ShapeDtypeStruct(w.shape[1:], BF16))

    tab = pl.BlockSpec((1, c, HEAD_DIM), lambda b, i: (b, i, 0))
    in_specs = [
        tab, tab, const((RET_HEADS, 3, c, c)),
        cols(*COL_RET), cols(*COL_HG), cols(*COL_ZX), cols(*COL_BC), cols(*COL_DT),
        const((DEPTH, HG_DIM)), lay((1, HG_DIM)),
        lay((M_CONV, M_CONV_DIM)), lay((1, M_CONV_DIM)), lay((1, LANES)), lay((1, LANES)), lay((1, M_DIM)),
        lay((1, M_DIM)), const((c, c)), const((LANES, M_DIM)),
        slab_in(w_out), slab_in(w1), slab_in(w2),
    ]
    args = [cos2, sin2, ret_tabs, proj, proj, proj, proj, proj, lb_logits, hg_norm_w3, conv_w, conv_b3, dtb3, alog3,
            dskip3, m_norm_w3, tri, expand, w_out, w1, w2]
    out_specs = [pl.BlockSpec((1, c, D_MODEL), lambda b, i: (b, i, 0))]
    out_shape = [jax.ShapeDtypeStruct((bsz, seq, D_MODEL), BF16)]
    for w in (w_out, w1, w2):
        spec, shape = slab_out(w)
        out_specs.append(spec)
        out_shape.append(shape)
    if has_next:
        win_in, win_out, win_shape = _win_specs(layer + 1, n_steps, lambda b, i: b * n_chunks + i)
        in_specs.append(win_in)
        args.append(w_in_t)
        out_specs.append(win_out)
        out_shape.append(win_shape)
    return pl.pallas_call(
        functools.partial(_mixers_kernel, layer, has_next, n_chunks),
        grid=(bsz, n_chunks),
        in_specs=in_specs,
        out_specs=out_specs,
        out_shape=out_shape,
        scratch_shapes=[
            pltpu.VMEM((RET_HEADS, HEAD_DIM, HEAD_DIM), F32),
            pltpu.VMEM((HG_HEADS, HEAD_DIM, HEAD_DIM), F32),
            pltpu.VMEM((M_GROUPS, M_STATE, M_DIM // M_GROUPS), F32),
            pltpu.VMEM((SUBLANES, M_DIM), F32),
            pltpu.VMEM((SUBLANES, M_BC), F32),
        ],
        compiler_params=_params("arbitrary", "arbitrary"),
        name="mixers",
    )(*args)


def _outproj_kernel(x_ref, mix_ref, w_ref, ada_ref, g_ref, b_ref, o_ref):
    gate = ada_ref[0, 2:3, :]
    for r in range(0, o_ref.shape[1], OUTPROJ_TR):
        rs = slice(r, r + OUTPROJ_TR)
        y = _dot(mix_ref[0, rs, :], w_ref[...])
        o_ref[0, rs, :] = _layer_norm(DEEPNORM_ALPHA * x_ref[0, rs, :] + (1.0 + gate) * y,
                                      g_ref[0, 0:1, :], b_ref[0, 0:1, :])


def _outproj(x, mix, w_out_b, ada, ln_g, ln_b, layer):
    bsz, seq, _ = x.shape
    tm = min(OUTPROJ_TM, seq)
    rows = pl.BlockSpec((1, tm, D_MODEL), lambda b, m: (b, m, 0))
    return pl.pallas_call(
        _outproj_kernel,
        grid=(bsz, seq // tm),
        in_specs=[
            rows, rows,
            pl.BlockSpec((D_MODEL, D_MODEL), lambda b, m: (0, 0)),
            pl.BlockSpec((1, N_ADA, D_MODEL), lambda b, m: (layer * bsz + b, 0, 0)),
            pl.BlockSpec((1, 2, D_MODEL), lambda b, m: (layer, 0, 0)),
            pl.BlockSpec((1, 2, D_MODEL), lambda b, m: (layer, 0, 0)),
        ],
        out_specs=rows,
        out_shape=jax.ShapeDtypeStruct(x.shape, F32),
        compiler_params=_params("arbitrary", "arbitrary"),
        name="outproj",
    )(x, mix, w_out_b, ada, ln_g, ln_b)


def _mlp_kernel(x_ref, ada_ref, w1_ref, w2_ref, g_ref, b_ref, o_ref, h_ref):
    f = pl.program_id(2)
    last_f = pl.num_programs(2) - 1
    blocks = [slice(r, r + MLP_TR) for r in range(0, o_ref.shape[1], MLP_TR)]

    def step(first, last):
        def hidden(rs):
            if first:
                h_ref[rs, :] = (x_ref[0, rs, :] * (1.0 + ada_ref[0, 4:5, :]) + ada_ref[0, 3:4, :]).astype(BF16)
            u = jnp.maximum(_dot(h_ref[rs, :], w1_ref[...]), 0.0)
            return (u * u).astype(BF16)

        if last:
            u2 = {i: hidden(blocks[i]) for i in range(min(2, len(blocks)))}
            for i, rs in enumerate(blocks):
                y = jnp.concatenate([o_ref[0, rs, n:n + MLP_TN] + _dot(u2[i], w2_ref[:, n:n + MLP_TN])
                                     for n in range(0, D_MODEL, MLP_TN)], axis=1)
                if i + 2 < len(blocks):
                    u2[i + 2] = hidden(blocks[i + 2])
                o_ref[0, rs, :] = _layer_norm(DEEPNORM_ALPHA * x_ref[0, rs, :] + (1.0 + ada_ref[0, 5:6, :]) * y,
                                              g_ref[0, 1:2, :], b_ref[0, 1:2, :])
            return
        u2 = [hidden(rs) for rs in blocks]
        for rs, u2_r in zip(blocks, u2):
            for n in range(0, D_MODEL, MLP_TN):
                y = _dot(u2_r, w2_ref[:, n:n + MLP_TN])
                if first:
                    o_ref[0, rs, n:n + MLP_TN] = y
                else:
                    o_ref[0, rs, n:n + MLP_TN] += y

    pl.when(f == 0)(functools.partial(step, True, False))
    pl.when((f > 0) & (f < last_f))(functools.partial(step, False, False))
    pl.when(f == last_f)(functools.partial(step, False, True))


def _mlp(x, ada, w1_b, w2_b, ln_g, ln_b, layer):
    bsz, seq, _ = x.shape
    tm = min(MLP_TM, seq)
    tf = MLP_TF
    return pl.pallas_call(
        _mlp_kernel,
        grid=(bsz, seq // tm, D_FF // tf),
        in_specs=[
            pl.BlockSpec((1, tm, D_MODEL), lambda b, m, f: (b, m, 0)),
            pl.BlockSpec((1, N_ADA, D_MODEL), lambda b, m, f: (layer * bsz + b, 0, 0)),
            pl.BlockSpec((D_MODEL, tf), lambda b, m, f: (0, f)),
            pl.BlockSpec((tf, D_MODEL), lambda b, m, f: (f, 0)),
            pl.BlockSpec((1, 2, D_MODEL), lambda b, m, f: (layer, 0, 0)),
            pl.BlockSpec((1, 2, D_MODEL), lambda b, m, f: (layer, 0, 0)),
        ],
        out_specs=pl.BlockSpec((1, tm, D_MODEL), lambda b, m, f: (b, m, 0)),
        out_shape=jax.ShapeDtypeStruct(x.shape, F32),
        scratch_shapes=[pltpu.VMEM((tm, D_MODEL), BF16)],
        compiler_params=_params("arbitrary", "arbitrary", "arbitrary"),
        name="mlp",
    )(x, ada, w1_b, w2_b, ln_g, ln_b)


def kernel(x, c, positions, lb_logits, w_in, w_out, w_ada, b_ada, ln_g, ln_b, hg_norm_w, m_conv_w, m_conv_b,
           m_dt_bias, m_a_log, m_d, m_norm_w, w1, w2):
    bsz, seq, _ = x.shape
    assert seq % CHUNK == 0

    pad_heads = ((0, 0), (0, LANES - M_HEADS))
    dtb3 = jnp.pad(m_dt_bias, pad_heads).reshape(DEPTH, 1, LANES)
    alog3 = jnp.pad(m_a_log, pad_heads).reshape(DEPTH, 1, LANES)
    dskip3 = jnp.repeat(m_d, M_HEAD_DIM, axis=1).reshape(DEPTH, 1, M_DIM)
    conv_b3 = m_conv_b.reshape(DEPTH, 1, M_CONV_DIM)
    m_norm_w3 = m_norm_w.reshape(DEPTH, 1, M_DIM)
    hg_norm_w3 = hg_norm_w.reshape(DEPTH, 1, HG_DIM)
    pos3 = positions.reshape(bsz, seq, 1)

    inv_freq = 1.0 / (ROPE_BASE ** jnp.linspace(0.0, 1.0, HEAD_DIM // 2, dtype=F32))
    invf2 = jnp.concatenate([inv_freq, inv_freq]).reshape(1, HEAD_DIM)
    idx = jnp.arange(CHUNK)
    tri = (idx[None, :] <= idx[:, None]).astype(BF16)
    expand = (jnp.arange(M_DIM)[None, :] // M_HEAD_DIM == jnp.arange(LANES)[:, None]).astype(BF16)
    ret_tabs = _retention_tables()

    w_in_t = jnp.swapaxes(w_in, 1, 2)
    ada = _ada(c, w_ada, b_ada).reshape(DEPTH * bsz, N_ADA, D_MODEL)
    cos2, sin2, w_in_b = _rope(pos3, invf2, w_in_t)
    for l in range(DEPTH):
        proj = _inproj(x, ada, w_in_b, l)
        mix, w_out_b, w1_b, w2_b, *rest = _mixers(proj, cos2, sin2, ret_tabs, lb_logits, hg_norm_w3, m_conv_w,
                                                  conv_b3, dtb3, alog3, dskip3, m_norm_w3, tri, expand, w_out, w1,
                                                  w2, w_in_t, l)
        if rest:
            w_in_b = rest[0]
        x = _outproj(x, mix, w_out_b, ada, ln_g, ln_b, l)
        x = _mlp(x, ada, w1_b, w2_b, ln_g, ln_b, l)
    return x
```
